```python
import jax
import jax.numpy as jnp
from jax import lax
import numpy as np

D_MODEL = 1024
BATCH = 32
SEQ = 2048
DEPTH = 1
DEC_BATCH = 32
DEC_SEQ = 64
PAST_LEN = 2048

CHUNK = 64
RET_HEADS = 4
RET_DK = 128
RET_DV = 128
RET_WIDTH = RET_HEADS * RET_DV
ATT_HEADS = 8
ATT_KV_HEADS = 2
ATT_HD = 64
ATT_REP = ATT_HEADS // ATT_KV_HEADS
ATT_WIDTH = ATT_HEADS * ATT_HD
WINDOW = 128
BAND_CHUNKS = WINDOW // CHUNK
MIX_WIDTH = RET_WIDTH + ATT_WIDTH
D_FF = 2816
CONV_W = 3
EPS = 1e-6
IN_SIZES = (RET_HEADS * RET_DK, RET_HEADS * RET_DK, RET_WIDTH, RET_WIDTH,
            ATT_WIDTH, ATT_KV_HEADS * ATT_HD, ATT_KV_HEADS * ATT_HD)
IN_WIDTH = sum(IN_SIZES)

kernel_name = 'hymba_retention_swa_convffn_stream_step'


def rmsnorm(x, g):
    xf = x.astype(jnp.float32)
    y = xf * lax.rsqrt(jnp.mean(xf * xf, axis=-1, keepdims=True) + EPS)
    return (y * g.astype(jnp.float32)).astype(x.dtype)


def modulate(h, shift, scale):
    return h * (1 + scale[:, None, :]) + shift[:, None, :]


def split_in(proj):
    idx = np.cumsum(IN_SIZES)[:-1].tolist()
    return jnp.split(proj, idx, axis=-1)


def ret_log_gamma():
    return jnp.log1p(-jnp.exp2(-5.0 - jnp.arange(RET_HEADS, dtype=jnp.float32)))


def retention_block(q, k, v, state):
    L = q.shape[1]
    lg = ret_log_gamma()
    i = jnp.arange(L, dtype=jnp.float32)
    diff = i[:, None] - i[None, :]
    decay = jnp.where(diff >= 0, jnp.exp(lg[:, None, None] * jnp.maximum(diff, 0.0)), 0.0)
    scores = jnp.einsum('blhd,bmhd->bhlm', q, k) * decay[None]
    intra = jnp.einsum('bhlm,bmhe->blhe', scores, v)
    q_decay = jnp.exp(lg[None, :] * (i[:, None] + 1.0))
    cross = jnp.einsum('blhd,bhde->blhe', q, state) * q_decay[None, :, :, None]
    k_decay = jnp.exp(lg[None, :] * (L - 1.0 - i[:, None]))
    new_state = (state * jnp.exp(lg * L)[None, :, None, None]
                 + jnp.einsum('blhd,blhe->bhde', k * k_decay[None, :, :, None], v))
    return intra + cross, new_state


def retention_qkv(rq, rk, rv):
    B, L = rq.shape[:2]
    q = rq.reshape(B, L, RET_HEADS, RET_DK).astype(jnp.float32)
    k = rk.reshape(B, L, RET_HEADS, RET_DK).astype(jnp.float32) * (RET_DK ** -0.5)
    v = rv.reshape(B, L, RET_HEADS, RET_DV).astype(jnp.float32)
    return q, k, v


def retention_out(o, rg):
    B, L = o.shape[:2]
    o = o * lax.rsqrt(jnp.mean(o * o, axis=-1, keepdims=True) + EPS)
    return (jax.nn.silu(rg.astype(jnp.float32)) * o.reshape(B, L, RET_WIDTH)).astype(rg.dtype)


def retention_prompt(rq, rk, rv, rg):
    B, S = rq.shape[:2]
    nc = S // CHUNK
    q, k, v = retention_qkv(rq, rk, rv)

    def to_chunks(t):
        return t.reshape(B, nc, CHUNK, *t.shape[2:]).swapaxes(0, 1)

    def step(state, blk):
        out, state = retention_block(blk[0], blk[1], blk[2], state)
        return state, out

    state0 = jnp.zeros((B, RET_HEADS, RET_DK, RET_DV), jnp.float32)
    state, outs = lax.scan(step, state0, (to_chunks(q), to_chunks(k), to_chunks(v)))
    o = outs.swapaxes(0, 1).reshape(B, S, RET_HEADS, RET_DV)
    return retention_out(o, rg), state


def retention_sample(rq, rk, rv, rg, state):
    q, k, v = retention_qkv(rq, rk, rv)
    o, new_state = retention_block(q, k, v, state.astype(jnp.float32))
    return retention_out(o, rg), new_state


def alibi_slopes():
    h = jnp.arange(1, ATT_HEADS + 1, dtype=jnp.float32)
    return jnp.exp2(-8.0 * h / ATT_HEADS)


def band_attention(q, k, v, sinks, q_pos, k_pos, k_valid):
    s = jnp.einsum('bnqgrd,bnkgd->bngrqk', q, k).astype(jnp.float32) * (ATT_HD ** -0.5)
    dist = jnp.abs(q_pos[:, :, None] - k_pos[:, None, :]).astype(jnp.float32)
    slopes = alibi_slopes().reshape(ATT_KV_HEADS, ATT_REP)
    s = s - slopes[None, None, :, :, None, None] * dist[None, :, None, None, :, :]
    s = jnp.where(k_valid[None, :, None, None, None, :], s, -jnp.inf)
    sink = sinks.astype(jnp.float32).reshape(ATT_KV_HEADS, ATT_REP)[None, None, :, :, None, None]
    m = jnp.maximum(jnp.max(s, axis=-1, keepdims=True), sink)
    p = jnp.exp(s - m)
    p = p / (jnp.sum(p, axis=-1, keepdims=True) + jnp.exp(sink - m))
    return jnp.einsum('bngrqk,bnkgd->bnqgrd', p.astype(v.dtype), v)


def band_view(t, nc):
    B = t.shape[0]
    tp = jnp.pad(t, ((0, 0), (BAND_CHUNKS * CHUNK, 0), (0, 0), (0, 0)))
    tc = tp.reshape(B, nc + BAND_CHUNKS, CHUNK, ATT_KV_HEADS, ATT_HD)
    return jnp.concatenate([tc[:, j:j + nc] for j in range(BAND_CHUNKS + 1)], axis=2)


def attention_prompt(aq, ak, av, sinks):
    B, S = aq.shape[:2]
    nc = S // CHUNK
    q = aq.reshape(B, nc, CHUNK, ATT_KV_HEADS, ATT_REP, ATT_HD)
    k = ak.reshape(B, S, ATT_KV_HEADS, ATT_HD)
    v = av.reshape(B, S, ATT_KV_HEADS, ATT_HD)
    q_pos = jnp.arange(S).reshape(nc, CHUNK)
    k_pos = ((jnp.arange(nc)[:, None] - BAND_CHUNKS) * CHUNK
             + jnp.arange((BAND_CHUNKS + 1) * CHUNK)[None, :])
    o = band_attention(q, band_view(k, nc), band_view(v, nc), sinks, q_pos, k_pos, k_pos >= 0)
    w = min(WINDOW, S)
    return o.reshape(B, S, ATT_WIDTH), k[:, S - w:], v[:, S - w:]


def attention_sample(aq, ak, av, sinks, cache_k, cache_v):
    B, L = aq.shape[:2]
    wc = cache_k.shape[1]
    q = aq.reshape(B, 1, L, ATT_KV_HEADS, ATT_REP, ATT_HD)
    k = ak.reshape(B, L, ATT_KV_HEADS, ATT_HD)
    v = av.reshape(B, L, ATT_KV_HEADS, ATT_HD)
    kk = jnp.concatenate([cache_k.astype(k.dtype), k], axis=1)[:, None]
    vv = jnp.concatenate([cache_v.astype(v.dtype), v], axis=1)[:, None]
    q_pos = (PAST_LEN + jnp.arange(L))[None, :]
    k_pos = (PAST_LEN - wc + jnp.arange(wc + L))[None, :]
    o = band_attention(q, kk, vv, sinks, q_pos, k_pos, k_pos >= 0)
    return o.reshape(B, L, ATT_WIDTH), k, v


def conv_ffn(h, w_up, conv_w, conv_b, w_down, conv_state):
    L = h.shape[1]
    a, b = jnp.split(h @ w_up, 2, axis=-1)
    ext = jnp.concatenate([conv_state.astype(a.dtype), a], axis=1)
    conv = conv_b[None, None, :] + sum(ext[:, j:j + L] * conv_w[j] for j in range(CONV_W))
    y = (jax.nn.silu(conv) * b) @ w_down
    return y, ext[:, L:]


def trunk_layer(x, c, ln1, ln2, w_ada, b_ada, w_in, w_out, sinks, w_up, conv_w, conv_b, w_down,
                cache_k=None, cache_v=None, ret_state=None, conv_state=None):
    is_prompt = cache_k is None
    B = x.shape[0]
    mod = jax.nn.silu(c) @ w_ada + b_ada
    sh1, sc1, g1, sh2, sc2, g2 = jnp.split(mod, 6, axis=-1)
    h = modulate(rmsnorm(x, ln1), sh1, sc1)
    rq, rk, rv, rg, aq, ak, av = split_in(h @ w_in)
    if is_prompt:
        ret_o, ret_new = retention_prompt(rq, rk, rv, rg)
        att_o, k_rows, v_rows = attention_prompt(aq, ak, av, sinks)
        conv_state = jnp.zeros((B, CONV_W - 1, D_FF), x.dtype)
    else:
        ret_o, ret_new = retention_sample(rq, rk, rv, rg, ret_state)
        att_o, k_rows, v_rows = attention_sample(aq, ak, av, sinks, cache_k, cache_v)
    mix = jnp.concatenate([ret_o, att_o], axis=-1) @ w_out
    x = x + g1[:, None, :] * mix
    h2 = modulate(rmsnorm(x, ln2), sh2, sc2)
    ffn, conv_new = conv_ffn(h2, w_up, conv_w, conv_b, w_down, conv_state)
    x = x + g2[:, None, :] * ffn
    return x, ret_new.astype(x.dtype), k_rows, v_rows, conv_new


def setup_inputs(seed: int = 0) -> dict:
    key = jax.random.key(seed)
    ks = jax.random.split(key, 24)
    f32 = jnp.float32

    def nrm(k, shape, s):
        return jax.random.normal(k, shape, f32) * s

    wc = min(WINDOW, PAST_LEN)
    return {
        'x_prompt': nrm(ks[0], (BATCH, SEQ, D_MODEL), 1.0),
        'x_sample': nrm(ks[1], (DEC_BATCH, DEC_SEQ, D_MODEL), 1.0),
        'cache_k': nrm(ks[2], (DEPTH, DEC_BATCH, wc, ATT_KV_HEADS, ATT_HD), 1.0),
        'cache_v': nrm(ks[3], (DEPTH, DEC_BATCH, wc, ATT_KV_HEADS, ATT_HD), 1.0),
        'state_ret': nrm(ks[4], (DEPTH, DEC_BATCH, RET_HEADS, RET_DK, RET_DV), 0.5),
        'state_conv': nrm(ks[5], (DEPTH, DEC_BATCH, CONV_W - 1, D_FF), 1.0),
        'c_prompt': nrm(ks[6], (BATCH, D_MODEL), 1.0),
        'c_sample': nrm(ks[7], (DEC_BATCH, D_MODEL), 1.0),
        'norm1_g': 1.0 + nrm(ks[8], (DEPTH, D_MODEL), 0.05),
        'norm2_g': 1.0 + nrm(ks[9], (DEPTH, D_MODEL), 0.05),
        'w_ada': nrm(ks[10], (DEPTH, D_MODEL, 6 * D_MODEL), 0.5 * D_MODEL ** -0.5),
        'b_ada': nrm(ks[11], (DEPTH, 6 * D_MODEL), 0.01),
        'w_in': nrm(ks[12], (DEPTH, D_MODEL, IN_WIDTH), D_MODEL ** -0.5),
        'w_out': nrm(ks[13], (DEPTH, MIX_WIDTH, D_MODEL), MIX_WIDTH ** -0.5),
        'attn_sinks': nrm(ks[14], (DEPTH, ATT_HEADS), 0.5),
        'w_up': nrm(ks[15], (DEPTH, D_MODEL, 2 * D_FF), D_MODEL ** -0.5),
        'conv_w': nrm(ks[16], (DEPTH, CONV_W, D_FF), CONV_W ** -0.5),
        'conv_b': nrm(ks[17], (DEPTH, D_FF), 0.01),
        'w_down': nrm(ks[18], (DEPTH, D_FF, D_MODEL), D_FF ** -0.5),
        'normf_g': 1.0 + nrm(ks[19], (D_MODEL,), 0.05),
        'w_ada_f': nrm(ks[20], (D_MODEL, 2 * D_MODEL), 0.5 * D_MODEL ** -0.5),
        'b_ada_f': nrm(ks[21], (2 * D_MODEL,), 0.01),
    }


def reference(x_prompt, x_sample, cache_k, cache_v, state_ret, state_conv, c_prompt, c_sample,
              norm1_g, norm2_g, w_ada, b_ada, w_in, w_out, attn_sinks, w_up, conv_w, conv_b,
              w_down, normf_g, w_ada_f, b_ada_f):
    yp, ys = x_prompt, x_sample
    p_ret, p_k, p_v, p_conv = [], [], [], []
    s_ret, s_k, s_v, s_conv = [], [], [], []
    for l in range(DEPTH):
        yp, r, kr, vr, cv = trunk_layer(yp, c_prompt, norm1_g[l], norm2_g[l], w_ada[l], b_ada[l],
                                        w_in[l], w_out[l], attn_sinks[l], w_up[l], conv_w[l],
                                        conv_b[l], w_down[l])
        p_ret.append(r); p_k.append(kr); p_v.append(vr); p_conv.append(cv)
        ys, r, kr, vr, cv = trunk_layer(ys, c_sample, norm1_g[l], norm2_g[l], w_ada[l], b_ada[l],
                                        w_in[l], w_out[l], attn_sinks[l], w_up[l], conv_w[l],
                                        conv_b[l], w_down[l], cache_k=cache_k[l], cache_v=cache_v[l],
                                        ret_state=state_ret[l], conv_state=state_conv[l])
        s_ret.append(r); s_k.append(kr); s_v.append(vr); s_conv.append(cv)
    shp, scp = jnp.split(jax.nn.silu(c_prompt) @ w_ada_f + b_ada_f, 2, axis=-1)
    shs, scs = jnp.split(jax.nn.silu(c_sample) @ w_ada_f + b_ada_f, 2, axis=-1)
    y_prompt = modulate(rmsnorm(yp, normf_g), shp, scp)
    y_sample = modulate(rmsnorm(ys, normf_g), shs, scs)
    p_state_ret = jnp.stack(p_ret)
    p_k_rows = jnp.stack(p_k)
    p_v_rows = jnp.stack(p_v)
    p_state_conv = jnp.stack(p_conv)
    s_state_ret = jnp.stack(s_ret)
    s_k_rows = jnp.stack(s_k)
    s_v_rows = jnp.stack(s_v)
    s_state_conv = jnp.stack(s_conv)
    return (y_prompt, y_sample, p_state_ret, p_k_rows, p_v_rows, p_state_conv,
            s_state_ret, s_k_rows, s_v_rows, s_state_conv)
```

```python
import functools

import jax
import jax.numpy as jnp
from jax import lax
from jax.experimental import pallas as pl
from jax.experimental.pallas import tpu as pltpu

F32 = jnp.float32
BF16 = jnp.bfloat16

CHUNK = 64
RET_HEADS = 4
RET_DK = 128
RET_DV = 128
RET_WIDTH = RET_HEADS * RET_DV
ATT_HEADS = 8
ATT_KV_HEADS = 2
ATT_HD = 64
ATT_REP = ATT_HEADS // ATT_KV_HEADS
ATT_WIDTH = ATT_HEADS * ATT_HD
KV_WIDTH = ATT_KV_HEADS * ATT_HD
WINDOW = 128
CONV_W = 3
EPS = 1e-6
CONV_PAD = 8
VMEM_LIMIT = 56 * 1024 * 1024

_OFF_RQ = 0
_OFF_RK = _OFF_RQ + RET_HEADS * RET_DK
_OFF_RV = _OFF_RK + RET_HEADS * RET_DK
_OFF_RG = _OFF_RV + RET_WIDTH
_OFF_AQ = _OFF_RG + RET_WIDTH
_OFF_AK = _OFF_AQ + ATT_WIDTH
_OFF_END = _OFF_AK + 2 * KV_WIDTH

_NT = (((1,), (1,)), ((), ()))
_TN = (((0,), (0,)), ((), ()))


def _silu(v):
    return v / (1.0 + jnp.exp(-v))


def _rms(v):
    return v * lax.rsqrt(jnp.mean(v * v, axis=-1, keepdims=True) + EPS)


def _ada_kernel(c_ref, w_ref, b_ref, o_ref):
    s = _silu(c_ref[...]).astype(BF16)
    o_ref[...] = jnp.dot(s, w_ref[...].astype(BF16), preferred_element_type=F32) + b_ref[...]


def _ada(c, w, b, tn=1024):
    m, d = c.shape
    n = w.shape[1]
    return pl.pallas_call(
        _ada_kernel,
        grid=(n // tn,),
        in_specs=[pl.BlockSpec((m, d), lambda i: (0, 0)),
                  pl.BlockSpec((d, tn), lambda i: (0, i)),
                  pl.BlockSpec((1, tn), lambda i: (0, i))],
        out_specs=pl.BlockSpec((m, tn), lambda i: (0, i)),
        out_shape=jax.ShapeDtypeStruct((m, n), F32),
        name="ada",
    )(c, w, b.reshape(1, n))


def _mixer_kernel(*refs, has_hist, T, RC, QG):
    if has_hist:
        (x_ref, mod_ref, ln_ref, win_ref, wout_ref, sink_ref, bias_ref, dmat_ref, qdec_ref, kdec_ref,
         gl_ref, ck_ref, cv_ref, st_ref,
         x1_ref, ret_ref, krow_ref, vrow_ref,
         state_s, kbuf, vbuf, mixbuf) = refs
    else:
        (x_ref, mod_ref, ln_ref, win_ref, wout_ref, sink_ref, bias_ref, dmat_ref, qdec_ref, kdec_ref,
         gl_ref,
         x1_ref, ret_ref, krow_ref, vrow_ref,
         state_s, kbuf, vbuf, mixbuf) = refs
    j = pl.program_id(1)
    nj = pl.num_programs(1)
    H = WINDOW

    @pl.when(j == 0)
    def _():
        if has_hist:
            state_s[...] = st_ref[...]
            kbuf[0:H, :] = ck_ref[...].astype(BF16)
            vbuf[0:H, :] = cv_ref[...].astype(BF16)
        else:
            state_s[...] = jnp.zeros(state_s.shape, F32)
            kbuf[0:H, :] = jnp.zeros((H, KV_WIDTH), BF16)
            vbuf[0:H, :] = jnp.zeros((H, KV_WIDTH), BF16)

    if T >= H:
        @pl.when(j > 0)
        def _():
            kbuf[0:H, :] = kbuf[T:T + H, :]
            vbuf[0:H, :] = vbuf[T:T + H, :]

    x = x_ref[...]
    h = _rms(x) * ln_ref[...]
    h = h * (1.0 + mod_ref[1:2, :]) + mod_ref[0:1, :]
    hb = h.astype(BF16)

    def proj(lo, hi):
        return jnp.dot(hb, win_ref[:, lo:hi], preferred_element_type=F32)

    rq = proj(_OFF_RQ, _OFF_RK).astype(BF16)
    rk = proj(_OFF_RK, _OFF_RV)
    rv = proj(_OFF_RV, _OFF_RG).astype(BF16)
    rg = proj(_OFF_RG, _OFF_AQ)
    aq = (proj(_OFF_AQ, _OFF_AK) * (ATT_HD ** -0.5)).astype(BF16)
    akv = proj(_OFF_AK, _OFF_END)
    ak = akv[:, :KV_WIDTH]
    av = akv[:, KV_WIDTH:]
    kbuf[H:H + T, :] = ak.astype(BF16)
    vbuf[H:H + T, :] = av.astype(BF16)
    if has_hist:
        krow_ref[...] = ak
        vrow_ref[...] = av
    else:
        @pl.when(j == nj - 1)
        def _():
            krow_ref[...] = ak[T - H:T, :]
            vrow_ref[...] = av[T - H:T, :]

    for c in range(T // RC):
        r0 = c * RC
        for hh in range(RET_HEADS):
            cs = slice(hh * RET_DK, (hh + 1) * RET_DK)
            q = rq[r0:r0 + RC, cs]
            kf = rk[r0:r0 + RC, cs]
            v = rv[r0:r0 + RC, cs]
            s = lax.dot_general(q, kf.astype(BF16), _NT, preferred_element_type=F32) * dmat_ref[hh]
            st = state_s[hh]
            o = (jnp.dot(s.astype(BF16), v, preferred_element_type=F32)
                 + jnp.dot(q, st.astype(BF16), preferred_element_type=F32) * qdec_ref[hh])
            kd = (kf * kdec_ref[hh]).astype(BF16)
            state_s[hh] = st * gl_ref[hh] + lax.dot_general(kd, v, _TN, preferred_element_type=F32)
            g = rg[r0:r0 + RC, cs]
            mixbuf[r0:r0 + RC, cs] = (_silu(g) * _rms(o)).astype(BF16)

    for p in range(T // QG):
        q0 = p * QG
        for g in range(ATT_KV_HEADS):
            q4 = jnp.concatenate(
                [aq[q0:q0 + QG, (g * ATT_REP + r) * ATT_HD:(g * ATT_REP + r + 1) * ATT_HD]
                 for r in range(ATT_REP)], axis=0)
            kb = kbuf[q0:q0 + QG + H, g * ATT_HD:(g + 1) * ATT_HD]
            vb = vbuf[q0:q0 + QG + H, g * ATT_HD:(g + 1) * ATT_HD]
            if (not has_hist) and p == 0:
                bias = bias_ref[g + ATT_KV_HEADS * (j == 0).astype(jnp.int32)]
            else:
                bias = bias_ref[g]
            s = lax.dot_general(q4, kb, _NT, preferred_element_type=F32) + bias
            sk = sink_ref[g]
            m = jnp.maximum(jnp.max(s, axis=-1, keepdims=True), sk)
            pe = jnp.exp(s - m)
            den = jnp.sum(pe, axis=-1, keepdims=True) + jnp.exp(sk - m)
            o = jnp.dot(pe.astype(BF16), vb, preferred_element_type=F32) / den
            for r in range(ATT_REP):
                c0 = RET_WIDTH + (g * ATT_REP + r) * ATT_HD
                mixbuf[q0:q0 + QG, c0:c0 + ATT_HD] = o[r * QG:(r + 1) * QG, :].astype(BF16)

    mix = jnp.dot(mixbuf[...], wout_ref[...], preferred_element_type=F32)
    x1_ref[...] = x + mod_ref[2:3, :] * mix

    @pl.when(j == nj - 1)
    def _():
        ret_ref[...] = state_s[...]


def _ret_log_gamma():
    return jnp.log1p(-jnp.exp2(-5.0 - jnp.arange(RET_HEADS, dtype=F32)))


def _retention_consts(RC):
    lg = _ret_log_gamma()
    i = jnp.arange(RC, dtype=F32)
    diff = i[:, None] - i[None, :]
    scale = RET_DK ** -0.5
    dmat = jnp.where(diff >= 0, jnp.exp(lg[:, None, None] * jnp.maximum(diff, 0.0)), 0.0) * scale
    qdec = jnp.exp(lg[:, None] * (i[None, :] + 1.0))
    kdec = jnp.exp(lg[:, None] * (RC - 1.0 - i[None, :])) * scale
    gl = jnp.exp(lg * RC)
    bc = lambda t: jnp.broadcast_to(t[:, :, None], (RET_HEADS, t.shape[1], RET_DV))
    return dmat, bc(qdec), bc(kdec), jnp.broadcast_to(gl[:, None, None], (RET_HEADS, 1, RET_DV))


def _attention_bias(QG, with_first):
    KB = QG + WINDOW
    qi = jnp.arange(QG)[:, None]
    ki = jnp.arange(KB)[None, :] - WINDOW
    qc = qi // CHUNK
    kc = jnp.floor_divide(ki, CHUNK)
    vis = (kc <= qc) & (kc >= qc - WINDOW // CHUNK)
    dist = jnp.abs(qi - ki).astype(F32)
    slopes = jnp.exp2(-8.0 * jnp.arange(1, ATT_HEADS + 1, dtype=F32) / ATT_HEADS)
    b = -slopes[:, None, None] * dist[None]
    b = jnp.where(vis[None], b, -jnp.inf)
    out = b.reshape(ATT_KV_HEADS, ATT_REP * QG, KB)
    if with_first:
        bf = jnp.where((ki >= 0)[None], b, -jnp.inf).reshape(ATT_KV_HEADS, ATT_REP * QG, KB)
        out = jnp.concatenate([out, bf], axis=0)
    return out


def _const_spec(shape):
    nd = len(shape)
    return pl.BlockSpec(shape, lambda b, j: (0,) * nd)


def _mixer(x, mod, ln, w_in, w_out, sinks, hist, T, RC, QG):
    B, S, D = x.shape
    has_hist = hist is not None
    nT = S // T
    H = WINDOW
    dmat, qdec, kdec, gl = _retention_consts(RC)
    bias = _attention_bias(QG, not has_hist)
    sinkcol = jnp.repeat(sinks.astype(F32).reshape(ATT_KV_HEADS, ATT_REP, 1), QG, axis=2)
    sinkcol = sinkcol.reshape(ATT_KV_HEADS, ATT_REP * QG, 1)
    W = T if has_hist else H

    in_specs = [
        pl.BlockSpec((None, T, D), lambda b, j: (b, j, 0)),
        pl.BlockSpec((None, 6, D), lambda b, j: (b, 0, 0)),
        _const_spec((1, D)),
        _const_spec(w_in.shape),
        _const_spec(w_out.shape),
        _const_spec(sinkcol.shape),
        _const_spec(bias.shape),
        _const_spec(dmat.shape),
        _const_spec(qdec.shape),
        _const_spec(kdec.shape),
        _const_spec(gl.shape),
    ]
    args = [x, mod, ln.reshape(1, D), w_in, w_out, sinkcol, bias, dmat, qdec, kdec, gl]
    if has_hist:
        ck, cv, st = hist
        in_specs += [
            pl.BlockSpec((None, H, KV_WIDTH), lambda b, j: (b, 0, 0)),
            pl.BlockSpec((None, H, KV_WIDTH), lambda b, j: (b, 0, 0)),
            pl.BlockSpec((None, RET_HEADS, RET_DK, RET_DV), lambda b, j: (b, 0, 0, 0)),
        ]
        args += [ck, cv, st]
    out_specs = [
        pl.BlockSpec((None, T, D), lambda b, j: (b, j, 0)),
        pl.BlockSpec((None, RET_HEADS, RET_DK, RET_DV), lambda b, j: (b, 0, 0, 0)),
        pl.BlockSpec((None, W, KV_WIDTH), lambda b, j: (b, 0, 0)),
        pl.BlockSpec((None, W, KV_WIDTH), lambda b, j: (b, 0, 0)),
    ]
    out_shape = [
        jax.ShapeDtypeStruct((B, S, D), F32),
        jax.ShapeDtypeStruct((B, RET_HEADS, RET_DK, RET_DV), F32),
        jax.ShapeDtypeStruct((B, W, KV_WIDTH), F32),
        jax.ShapeDtypeStruct((B, W, KV_WIDTH), F32),
    ]
    scratch = [
        pltpu.VMEM((RET_HEADS, RET_DK, RET_DV), F32),
        pltpu.VMEM((H + T, KV_WIDTH), BF16),
        pltpu.VMEM((H + T, KV_WIDTH), BF16),
        pltpu.VMEM((T, D), BF16),
    ]
    return pl.pallas_call(
        functools.partial(_mixer_kernel, has_hist=has_hist, T=T, RC=RC, QG=QG),
        grid=(B, nT),
        in_specs=in_specs,
        out_specs=out_specs,
        out_shape=out_shape,
        scratch_shapes=scratch,
        compiler_params=pltpu.CompilerParams(
            dimension_semantics=("parallel", "arbitrary"), vmem_limit_bytes=VMEM_LIMIT),
        name="mixer_sample" if has_hist else "mixer_prompt",
    )(*args)


def _ffn_kernel(*refs, has_hist, T, FC, final):
    if has_hist:
        (x_ref, mod_ref, ln_ref, wup_ref, cw_ref, cb_ref, wdn_ref, modf_ref, lnf_ref, cs_ref,
         y_ref, cso_ref, prev_s) = refs
    else:
        (x_ref, mod_ref, ln_ref, wup_ref, cw_ref, cb_ref, wdn_ref, modf_ref, lnf_ref,
         y_ref, cso_ref, prev_s) = refs
    j = pl.program_id(1)
    nj = pl.num_programs(1)
    D = x_ref.shape[-1]
    F = cw_ref.shape[-1]

    @pl.when(j == 0)
    def _():
        if has_hist:
            prev_s[...] = cs_ref[...]
        else:
            prev_s[...] = jnp.zeros(prev_s.shape, F32)

    x1 = x_ref[...]
    h = _rms(x1) * ln_ref[...]
    h = h * (1.0 + mod_ref[4:5, :]) + mod_ref[3:4, :]
    hb = h.astype(BF16)
    row = lax.broadcasted_iota(jnp.int32, (T, FC), 0)
    acc = jnp.zeros((T, D), F32)
    for c in range(F // FC):
        cs = slice(c * FC, (c + 1) * FC)
        a = jnp.dot(hb, wup_ref[:, c * FC:(c + 1) * FC], preferred_element_type=F32)
        b = jnp.dot(hb, wup_ref[:, F + c * FC:F + (c + 1) * FC], preferred_element_type=F32)
        pv = prev_s[:, cs]
        p1 = pv[CONV_PAD - 1:CONV_PAD, :]
        p2 = pv[CONV_PAD - 2:CONV_PAD - 1, :]
        a1 = jnp.where(row == 0, p1, pltpu.roll(a, 1, 0))
        a2 = jnp.where(row == 0, p2, jnp.where(row == 1, p1, pltpu.roll(a, 2, 0)))
        cw = cw_ref[:, cs]
        conv = cb_ref[:, cs] + (a2 * cw[0:1, :] + a1 * cw[1:2, :] + a * cw[2:3, :])
        gate = (_silu(conv) * b).astype(BF16)
        acc = acc + jnp.dot(gate, wdn_ref[c * FC:(c + 1) * FC, :], preferred_element_type=F32)
        prev_s[:, cs] = a[T - CONV_PAD:T, :]
    x2 = x1 + mod_ref[5:6, :] * acc
    if final:
        x2 = (_rms(x2) * lnf_ref[...]) * (1.0 + modf_ref[1:2, :]) + modf_ref[0:1, :]
    y_ref[...] = x2

    @pl.when(j == nj - 1)
    def _():
        cso_ref[...] = prev_s[...]


def _ffn(x1, mod, ln, w_up, conv_w, conv_b, w_down, modf, lnf, conv_state, T, FC, final):
    B, S, D = x1.shape
    F = conv_w.shape[-1]
    has_hist = conv_state is not None
    in_specs = [
        pl.BlockSpec((None, T, D), lambda b, j: (b, j, 0)),
        pl.BlockSpec((None, 6, D), lambda b, j: (b, 0, 0)),
        _const_spec((1, D)),
        _const_spec(w_up.shape),
        _const_spec(conv_w.shape),
        _const_spec((1, F)),
        _const_spec(w_down.shape),
        pl.BlockSpec((None, 2, D), lambda b, j: (b, 0, 0)),
        _const_spec((1, D)),
    ]
    args = [x1, mod, ln.reshape(1, D), w_up, conv_w, conv_b.reshape(1, F), w_down, modf, lnf.reshape(1, D)]
    if has_hist:
        in_specs.append(pl.BlockSpec((None, CONV_PAD, F), lambda b, j: (b, 0, 0)))
        args.append(jnp.pad(conv_state, ((0, 0), (CONV_PAD - (CONV_W - 1), 0), (0, 0))))
    y, cso = pl.pallas_call(
        functools.partial(_ffn_kernel, has_hist=has_hist, T=T, FC=FC, final=final),
        grid=(B, S // T),
        in_specs=in_specs,
        out_specs=[pl.BlockSpec((None, T, D), lambda b, j: (b, j, 0)),
                   pl.BlockSpec((None, CONV_PAD, F), lambda b, j: (b, 0, 0))],
        out_shape=[jax.ShapeDtypeStruct((B, S, D), F32),
                   jax.ShapeDtypeStruct((B, CONV_PAD, F), F32)],
        scratch_shapes=[pltpu.VMEM((CONV_PAD, F), F32)],
        compiler_params=pltpu.CompilerParams(
            dimension_semantics=("parallel", "arbitrary"), vmem_limit_bytes=VMEM_LIMIT),
        name="ffn_sample" if has_hist else "ffn_prompt",
    )(*args)
    return y, cso[:, CONV_PAD - (CONV_W - 1):, :]


def _pick_tile(S, pref):
    t = min(S, pref)
    while S % t:
        t //= 2
    return t


def kernel(x_prompt, x_sample, cache_k, cache_v, state_ret, state_conv, c_prompt, c_sample, norm1_g, norm2_g, w_ada, b_ada, w_in, w_out, attn_sinks, w_up, conv_w, conv_b, w_down, normf_g, w_ada_f, b_ada_f):
    depth = w_in.shape[0]
    B, S, D = x_prompt.shape
    Bs, Ls, _ = x_sample.shape
    assert S % WINDOW == 0 and Ls == CHUNK and cache_k.shape[2] == WINDOW

    c_all = jnp.concatenate([c_prompt, c_sample], axis=0)
    modf = _ada(c_all, w_ada_f, b_ada_f).reshape(B + Bs, 2, D)
    Tm = _pick_tile(S, 512)
    Tf = _pick_tile(S, 512)
    RCp = _pick_tile(Tm, 256)

    yp, ys = x_prompt, x_sample
    outs_p, outs_s = [], []
    for l in range(depth):
        mod = _ada(c_all, w_ada[l], b_ada[l]).reshape(B + Bs, 6, D)
        mod_p, mod_s = mod[:B], mod[B:]
        win = w_in[l].astype(BF16)
        wout = w_out[l].astype(BF16)
        wup = w_up[l].astype(BF16)
        wdn = w_down[l].astype(BF16)
        final = l == depth - 1

        x1, ret_p, k_p, v_p = _mixer(yp, mod_p, norm1_g[l], win, wout, attn_sinks[l], None,
                                     T=Tm, RC=RCp, QG=WINDOW)
        yp, conv_p = _ffn(x1, mod_p, norm2_g[l], wup, conv_w[l], conv_b[l], wdn, modf[:B], normf_g,
                          None, T=Tf, FC=256, final=final)
        outs_p.append((ret_p, k_p.reshape(B, WINDOW, ATT_KV_HEADS, ATT_HD),
                       v_p.reshape(B, WINDOW, ATT_KV_HEADS, ATT_HD), conv_p))

        hist = (cache_k[l].reshape(Bs, WINDOW, KV_WIDTH), cache_v[l].reshape(Bs, WINDOW, KV_WIDTH),
                state_ret[l])
        x1s, ret_s, k_s, v_s = _mixer(ys, mod_s, norm1_g[l], win, wout, attn_sinks[l], hist,
                                      T=Ls, RC=Ls, QG=Ls)
        ys, conv_s = _ffn(x1s, mod_s, norm2_g[l], wup, conv_w[l], conv_b[l], wdn, modf[B:], normf_g,
                          state_conv[l], T=Ls, FC=256, final=final)
        outs_s.append((ret_s, k_s.reshape(Bs, Ls, ATT_KV_HEADS, ATT_HD),
                       v_s.reshape(Bs, Ls, ATT_KV_HEADS, ATT_HD), conv_s))

    stack = lambda lst, i: jnp.stack([t[i] for t in lst])
    return (yp, ys,
            stack(outs_p, 0), stack(outs_p, 1), stack(outs_p, 2), stack(outs_p, 3),
            stack(outs_s, 0), stack(outs_s, 1), stack(outs_s, 2), stack(outs_s, 3))
```

```python
import functools

import jax
import jax.numpy as jnp
from jax import lax
from jax.experimental import pallas as pl
from jax.experimental.pallas import tpu as pltpu

F32 = jnp.float32
BF16 = jnp.bfloat16

CHUNK = 64
RET_HEADS = 4
RET_DK = 128
RET_DV = 128
RET_WIDTH = RET_HEADS * RET_DV
ATT_HEADS = 8
ATT_KV_HEADS = 2
ATT_HD = 64
ATT_REP = ATT_HEADS // ATT_KV_HEADS
ATT_WIDTH = ATT_HEADS * ATT_HD
KV_WIDTH = ATT_KV_HEADS * ATT_HD
WINDOW = 128
CONV_W = 3
EPS = 1e-6
CONV_PAD = 8
VMEM_LIMIT = 56 * 1024 * 1024
ATT_DEPTH = 2
OUT_CHUNK = 256

_OFF_RQ = 0
_OFF_RK = _OFF_RQ + RET_HEADS * RET_DK
_OFF_RV = _OFF_RK + RET_HEADS * RET_DK
_OFF_RG = _OFF_RV + RET_WIDTH
_OFF_AQ = _OFF_RG + RET_WIDTH
_OFF_AK = _OFF_AQ + ATT_WIDTH
_OFF_END = _OFF_AK + 2 * KV_WIDTH

_NT = (((1,), (1,)), ((), ()))
_TN = (((0,), (0,)), ((), ()))


def _silu(v):
    return v / (1.0 + jnp.exp(-v))


def _rms(v):
    return v * lax.rsqrt(jnp.mean(v * v, axis=-1, keepdims=True) + EPS)


def _ada_kernel(c_ref, w_ref, b_ref, o_ref):
    s = _silu(c_ref[...]).astype(BF16)
    o_ref[...] = jnp.dot(s, w_ref[...].astype(BF16), preferred_element_type=F32) + b_ref[...]


def _ada(c, w, b, tn=1024):
    m, d = c.shape
    n = w.shape[1]
    return pl.pallas_call(
        _ada_kernel,
        grid=(n // tn,),
        in_specs=[pl.BlockSpec((m, d), lambda i: (0, 0)),
                  pl.BlockSpec((d, tn), lambda i: (0, i)),
                  pl.BlockSpec((1, tn), lambda i: (0, i))],
        out_specs=pl.BlockSpec((m, tn), lambda i: (0, i)),
        out_shape=jax.ShapeDtypeStruct((m, n), F32),
        name="ada",
    )(c, w, b.reshape(1, n))


def _mixer_kernel(*refs, has_hist, T, RC, QG):
    if has_hist:
        (x_ref, mod_ref, ln_ref, win_ref, wout_ref, sink_ref, bias_ref, dmat_ref, qdec_ref, kdec_ref,
         gl_ref, ck_ref, cv_ref, st_ref,
         x1_ref, ret_ref, krow_ref, vrow_ref,
         state_s, kbuf, vbuf, mixbuf) = refs
    else:
        (x_ref, mod_ref, ln_ref, win_ref, wout_ref, sink_ref, bias_ref, dmat_ref, qdec_ref, kdec_ref,
         gl_ref,
         x1_ref, ret_ref, krow_ref, vrow_ref,
         state_s, kbuf, vbuf, mixbuf) = refs
    j = pl.program_id(1)
    nj = pl.num_programs(1)
    H = WINDOW

    @pl.when(j == 0)
    def _():
        if has_hist:
            state_s[...] = st_ref[...]
            kbuf[0:H, :] = ck_ref[...].astype(BF16)
            vbuf[0:H, :] = cv_ref[...].astype(BF16)
        else:
            state_s[...] = jnp.zeros(state_s.shape, F32)
            kbuf[0:H, :] = jnp.zeros((H, KV_WIDTH), BF16)
            vbuf[0:H, :] = jnp.zeros((H, KV_WIDTH), BF16)

    if T >= H:
        @pl.when(j > 0)
        def _():
            kbuf[0:H, :] = kbuf[T:T + H, :]
            vbuf[0:H, :] = vbuf[T:T + H, :]

    x = x_ref[...]
    h = _rms(x) * ln_ref[...]
    h = h * (1.0 + mod_ref[1:2, :]) + mod_ref[0:1, :]
    hb = h.astype(BF16)

    def proj(lo, hi):
        return jnp.dot(hb, win_ref[:, lo:hi], preferred_element_type=F32)

    rq = proj(_OFF_RQ, _OFF_RK).astype(BF16)
    rk = proj(_OFF_RK, _OFF_RV)
    rv = proj(_OFF_RV, _OFF_RG).astype(BF16)

    def ret_scores(c, hh):
        rows = slice(c * RC, (c + 1) * RC)
        cs = slice(hh * RET_DK, (hh + 1) * RET_DK)
        q = rq[rows, cs]
        kf = rk[rows, cs]
        v = rv[rows, cs]
        s = lax.dot_general(q, kf.astype(BF16), _NT, preferred_element_type=F32)
        st = state_s[hh]
        cross = jnp.dot(q, st.astype(BF16), preferred_element_type=F32)
        kd = (kf * kdec_ref[hh]).astype(BF16)
        state_s[hh] = st * gl_ref[hh] + lax.dot_general(kd, v, _TN, preferred_element_type=F32)
        return rows, cs, hh, s, cross, v

    def ret_finish(ctx):
        rows, cs, hh, s, cross, v = ctx
        o = (jnp.dot((s * dmat_ref[hh]).astype(BF16), v, preferred_element_type=F32)
             + cross * qdec_ref[hh])
        mixbuf[rows, cs] = (_silu(rg[rows, cs]) * _rms(o)).astype(BF16)

    def att_scores(p, g):
        q0 = p * QG
        q4 = jnp.concatenate(
            [aq[q0:q0 + QG, (g * ATT_REP + r) * ATT_HD:(g * ATT_REP + r + 1) * ATT_HD]
             for r in range(ATT_REP)], axis=0)
        kb = kbuf[q0:q0 + QG + H, g * ATT_HD:(g + 1) * ATT_HD]
        vb = vbuf[q0:q0 + QG + H, g * ATT_HD:(g + 1) * ATT_HD]
        s = lax.dot_general(kb, q4, _NT, preferred_element_type=F32)
        return p, g, s, vb

    def att_finish(ctx):
        p, g, s, vb = ctx
        q0 = p * QG
        if (not has_hist) and p == 0:
            bias = bias_ref[g + ATT_KV_HEADS * (j == 0).astype(jnp.int32)]
        else:
            bias = bias_ref[g]
        s = s + bias
        sk = sink_ref[g]
        m = jnp.maximum(jnp.max(s, axis=0, keepdims=True), sk)
        pe = jnp.exp(s - m)
        den = jnp.sum(pe, axis=0, keepdims=True) + jnp.exp(sk - m)
        pn = (pe * (1.0 / den)).astype(BF16)
        o = lax.dot_general(pn, vb, _TN, preferred_element_type=F32)
        for r in range(ATT_REP):
            c0 = RET_WIDTH + (g * ATT_REP + r) * ATT_HD
            mixbuf[q0:q0 + QG, c0:c0 + ATT_HD] = o[r * QG:(r + 1) * QG, :].astype(BF16)

    def out_ret(n):
        return jnp.dot(mixbuf[:, :RET_WIDTH], wout_ref[:RET_WIDTH, n * OUT_CHUNK:(n + 1) * OUT_CHUNK],
                       preferred_element_type=F32)

    n_blk = T // RC
    n_oc = x_ref.shape[-1] // OUT_CHUNK
    att_items = [(p, g) for p in range(T // QG) for g in range(ATT_KV_HEADS)]
    ret_ctx = [ret_scores(0, hh) for hh in range(RET_HEADS)]
    rg = proj(_OFF_RG, _OFF_AQ)
    aq = (proj(_OFF_AQ, _OFF_AK) * (ATT_HD ** -0.5)).astype(BF16)
    akv = proj(_OFF_AK, _OFF_END)
    ak = akv[:, :KV_WIDTH]
    av = akv[:, KV_WIDTH:]
    kbuf[H:H + T, :] = ak.astype(BF16)
    vbuf[H:H + T, :] = av.astype(BF16)
    if has_hist:
        krow_ref[...] = ak
        vrow_ref[...] = av
    else:
        @pl.when(j == nj - 1)
        def _():
            krow_ref[...] = ak[T - H:T, :]
            vrow_ref[...] = av[T - H:T, :]

    att_ctx = []
    for c in range(n_blk):
        nxt = []
        for hh in range(RET_HEADS):
            if c + 1 < n_blk:
                nxt.append(ret_scores(c + 1, hh))
            elif len(att_ctx) < min(ATT_DEPTH, len(att_items)):
                att_ctx.append(att_scores(*att_items[len(att_ctx)]))
            ret_finish(ret_ctx[hh])
        ret_ctx = nxt

    o_ret = []
    for i in range(len(att_items)):
        if len(att_ctx) < len(att_items):
            att_ctx.append(att_scores(*att_items[len(att_ctx)]))
        if len(o_ret) < n_oc:
            o_ret.append(out_ret(len(o_ret)))
        att_finish(att_ctx[i])
    while len(o_ret) < n_oc:
        o_ret.append(out_ret(len(o_ret)))
    for n in range(n_oc):
        cols = slice(n * OUT_CHUNK, (n + 1) * OUT_CHUNK)
        mix = o_ret[n] + jnp.dot(mixbuf[:, RET_WIDTH:], wout_ref[RET_WIDTH:, cols], preferred_element_type=F32)
        x1_ref[:, cols] = x[:, cols] + mod_ref[2:3, cols] * mix

    @pl.when(j == nj - 1)
    def _():
        ret_ref[...] = state_s[...]


def _ret_log_gamma():
    return jnp.log1p(-jnp.exp2(-5.0 - jnp.arange(RET_HEADS, dtype=F32)))


def _retention_consts(RC):
    lg = _ret_log_gamma()
    i = jnp.arange(RC, dtype=F32)
    diff = i[:, None] - i[None, :]
    scale = RET_DK ** -0.5
    dmat = jnp.where(diff >= 0, jnp.exp(lg[:, None, None] * jnp.maximum(diff, 0.0)), 0.0) * scale
    qdec = jnp.exp(lg[:, None] * (i[None, :] + 1.0))
    kdec = jnp.exp(lg[:, None] * (RC - 1.0 - i[None, :])) * scale
    gl = jnp.exp(lg * RC)
    bc = lambda t: jnp.broadcast_to(t[:, :, None], (RET_HEADS, t.shape[1], RET_DV))
    return dmat, bc(qdec), bc(kdec), jnp.broadcast_to(gl[:, None, None], (RET_HEADS, 1, RET_DV))


def _attention_bias(QG, with_first):
    KB = QG + WINDOW
    qi = jnp.arange(QG)[:, None]
    ki = jnp.arange(KB)[None, :] - WINDOW
    qc = qi // CHUNK
    kc = jnp.floor_divide(ki, CHUNK)
    vis = (kc <= qc) & (kc >= qc - WINDOW // CHUNK)
    dist = jnp.abs(qi - ki).astype(F32)
    slopes = jnp.exp2(-8.0 * jnp.arange(1, ATT_HEADS + 1, dtype=F32) / ATT_HEADS)
    b = -slopes[:, None, None] * dist[None]
    b = jnp.where(vis[None], b, -jnp.inf)
    lay = lambda t: t.reshape(ATT_KV_HEADS, ATT_REP * QG, KB).swapaxes(1, 2)
    out = lay(b)
    if with_first:
        out = jnp.concatenate([out, lay(jnp.where((ki >= 0)[None], b, -jnp.inf))], axis=0)
    return out


def _const_spec(shape):
    nd = len(shape)
    return pl.BlockSpec(shape, lambda b, j: (0,) * nd, pipeline_mode=pl.Buffered(1))


def _mixer(x, mod, ln, w_in, w_out, sinks, hist, T, RC, QG):
    B, S, D = x.shape
    has_hist = hist is not None
    nT = S // T
    H = WINDOW
    dmat, qdec, kdec, gl = _retention_consts(RC)
    bias = _attention_bias(QG, not has_hist)
    sinkcol = jnp.repeat(sinks.astype(F32).reshape(ATT_KV_HEADS, ATT_REP, 1), QG, axis=2)
    sinkcol = sinkcol.reshape(ATT_KV_HEADS, 1, ATT_REP * QG)
    W = T if has_hist else H

    in_specs = [
        pl.BlockSpec((None, T, D), lambda b, j: (b, j, 0)),
        pl.BlockSpec((None, 6, D), lambda b, j: (b, 0, 0)),
        _const_spec((1, D)),
        _const_spec(w_in.shape),
        _const_spec(w_out.shape),
        _const_spec(sinkcol.shape),
        _const_spec(bias.shape),
        _const_spec(dmat.shape),
        _const_spec(qdec.shape),
        _const_spec(kdec.shape),
        _const_spec(gl.shape),
    ]
    args = [x, mod, ln.reshape(1, D), w_in, w_out, sinkcol, bias, dmat, qdec, kdec, gl]
    if has_hist:
        ck, cv, st = hist
        in_specs += [
            pl.BlockSpec((None, H, KV_WIDTH), lambda b, j: (b, 0, 0)),
            pl.BlockSpec((None, H, KV_WIDTH), lambda b, j: (b, 0, 0)),
            pl.BlockSpec((None, RET_HEADS, RET_DK, RET_DV), lambda b, j: (b, 0, 0, 0)),
        ]
        args += [ck, cv, st]
    out_specs = [
        pl.BlockSpec((None, T, D), lambda b, j: (b, j, 0)),
        pl.BlockSpec((None, RET_HEADS, RET_DK, RET_DV), lambda b, j: (b, 0, 0, 0)),
        pl.BlockSpec((None, W, KV_WIDTH), lambda b, j: (b, 0, 0)),
        pl.BlockSpec((None, W, KV_WIDTH), lambda b, j: (b, 0, 0)),
    ]
    out_shape = [
        jax.ShapeDtypeStruct((B, S, D), F32),
        jax.ShapeDtypeStruct((B, RET_HEADS, RET_DK, RET_DV), F32),
        jax.ShapeDtypeStruct((B, W, KV_WIDTH), F32),
        jax.ShapeDtypeStruct((B, W, KV_WIDTH), F32),
    ]
    scratch = [
        pltpu.VMEM((RET_HEADS, RET_DK, RET_DV), F32),
        pltpu.VMEM((H + T, KV_WIDTH), BF16),
        pltpu.VMEM((H + T, KV_WIDTH), BF16),
        pltpu.VMEM((T, D), BF16),
    ]
    return pl.pallas_call(
        functools.partial(_mixer_kernel, has_hist=has_hist, T=T, RC=RC, QG=QG),
        grid=(B, nT),
        in_specs=in_specs,
        out_specs=out_specs,
        out_shape=out_shape,
        scratch_shapes=scratch,
        compiler_params=pltpu.CompilerParams(
            dimension_semantics=("parallel", "arbitrary"), vmem_limit_bytes=VMEM_LIMIT),
        name="mixer_sample" if has_hist else "mixer_prompt",
    )(*args)


def _ffn_kernel(*refs, has_hist, T, FC, final):
    if has_hist:
        (x_ref, mod_ref, ln_ref, wup_ref, cw_ref, cb_ref, wdn_ref, modf_ref, lnf_ref, cs_ref,
         y_ref, cso_ref, prev_s) = refs
    else:
        (x_ref, mod_ref, ln_ref, wup_ref, cw_ref, cb_ref, wdn_ref, modf_ref, lnf_ref,
         y_ref, cso_ref, prev_s) = refs
    j = pl.program_id(1)
    nj = pl.num_programs(1)
    D = x_ref.shape[-1]
    F = cw_ref.shape[-1]

    @pl.when(j == 0)
    def _():
        if has_hist:
            prev_s[...] = cs_ref[...]
        else:
            prev_s[...] = jnp.zeros(prev_s.shape, F32)

    x1 = x_ref[...]
    h = _rms(x1) * ln_ref[...]
    h = h * (1.0 + mod_ref[4:5, :]) + mod_ref[3:4, :]
    hb = h.astype(BF16)
    row = lax.broadcasted_iota(jnp.int32, (T, FC), 0)
    acc = jnp.zeros((T, D), F32)

    def up(c):
        return (jnp.dot(hb, wup_ref[:, c * FC:(c + 1) * FC], preferred_element_type=F32),
                jnp.dot(hb, wup_ref[:, F + c * FC:F + (c + 1) * FC], preferred_element_type=F32))

    nc = F // FC
    ab_next = up(0)
    for c in range(nc):
        cs = slice(c * FC, (c + 1) * FC)
        a, b = ab_next
        if c + 1 < nc:
            ab_next = up(c + 1)
        pv = prev_s[:, cs]
        p1 = pv[CONV_PAD - 1:CONV_PAD, :]
        p2 = pv[CONV_PAD - 2:CONV_PAD - 1, :]
        a1 = jnp.where(row == 0, p1, pltpu.roll(a, 1, 0))
        a2 = jnp.where(row == 0, p2, jnp.where(row == 1, p1, pltpu.roll(a, 2, 0)))
        cw = cw_ref[:, cs]
        conv = cb_ref[:, cs] + (a2 * cw[0:1, :] + a1 * cw[1:2, :] + a * cw[2:3, :])
        gate = (_silu(conv) * b).astype(BF16)
        acc = acc + jnp.dot(gate, wdn_ref[c * FC:(c + 1) * FC, :], preferred_element_type=F32)
        prev_s[:, cs] = a[T - CONV_PAD:T, :]
    x2 = x1 + mod_ref[5:6, :] * acc
    if final:
        x2 = (_rms(x2) * lnf_ref[...]) * (1.0 + modf_ref[1:2, :]) + modf_ref[0:1, :]
    y_ref[...] = x2

    @pl.when(j == nj - 1)
    def _():
        cso_ref[...] = prev_s[...]


def _ffn(x1, mod, ln, w_up, conv_w, conv_b, w_down, modf, lnf, conv_state, T, FC, final):
    B, S, D = x1.shape
    F = conv_w.shape[-1]
    has_hist = conv_state is not None
    in_specs = [
        pl.BlockSpec((None, T, D), lambda b, j: (b, j, 0)),
        pl.BlockSpec((None, 6, D), lambda b, j: (b, 0, 0)),
        _const_spec((1, D)),
        _const_spec(w_up.shape),
        _const_spec(conv_w.shape),
        _const_spec((1, F)),
        _const_spec(w_down.shape),
        pl.BlockSpec((None, 2, D), lambda b, j: (b, 0, 0)),
        _const_spec((1, D)),
    ]
    args = [x1, mod, ln.reshape(1, D), w_up, conv_w, conv_b.reshape(1, F), w_down, modf, lnf.reshape(1, D)]
    if has_hist:
        in_specs.append(pl.BlockSpec((None, CONV_PAD, F), lambda b, j: (b, 0, 0)))
        args.append(jnp.pad(conv_state, ((0, 0), (CONV_PAD - (CONV_W - 1), 0), (0, 0))))
    y, cso = pl.pallas_call(
        functools.partial(_ffn_kernel, has_hist=has_hist, T=T, FC=FC, final=final),
        grid=(B, S // T),
        in_specs=in_specs,
        out_specs=[pl.BlockSpec((None, T, D), lambda b, j: (b, j, 0)),
                   pl.BlockSpec((None, CONV_PAD, F), lambda b, j: (b, 0, 0))],
        out_shape=[jax.ShapeDtypeStruct((B, S, D), F32),
                   jax.ShapeDtypeStruct((B, CONV_PAD, F), F32)],
        scratch_shapes=[pltpu.VMEM((CONV_PAD, F), F32)],
        compiler_params=pltpu.CompilerParams(
            dimension_semantics=("parallel", "arbitrary"), vmem_limit_bytes=VMEM_LIMIT),
        name="ffn_sample" if has_hist else "ffn_prompt",
    )(*args)
    return y, cso[:, CONV_PAD - (CONV_W - 1):, :]


def _pick_tile(S, pref):
    t = min(S, pref)
    while S % t:
        t //= 2
    return t


def kernel(x_prompt, x_sample, cache_k, cache_v, state_ret, state_conv, c_prompt, c_sample, norm1_g, norm2_g, w_ada, b_ada, w_in, w_out, attn_sinks, w_up, conv_w, conv_b, w_down, normf_g, w_ada_f, b_ada_f):
    depth = w_in.shape[0]
    B, S, D = x_prompt.shape
    Bs, Ls, _ = x_sample.shape
    assert S % WINDOW == 0 and Ls == CHUNK and cache_k.shape[2] == WINDOW

    c_all = jnp.concatenate([c_prompt, c_sample], axis=0)
    modf = _ada(c_all, w_ada_f, b_ada_f).reshape(B + Bs, 2, D)
    Tm = _pick_tile(S, 512)
    Tf = _pick_tile(S, 1024)
    RCp = _pick_tile(Tm, 256)

    yp, ys = x_prompt, x_sample
    outs_p, outs_s = [], []
    for l in range(depth):
        mod = _ada(c_all, w_ada[l], b_ada[l]).reshape(B + Bs, 6, D)
        mod_p, mod_s = mod[:B], mod[B:]
        win = w_in[l].astype(BF16)
        wout = w_out[l].astype(BF16)
        wup = w_up[l].astype(BF16)
        wdn = w_down[l].astype(BF16)
        final = l == depth - 1

        x1, ret_p, k_p, v_p = _mixer(yp, mod_p, norm1_g[l], win, wout, attn_sinks[l], None,
                                     T=Tm, RC=RCp, QG=WINDOW)
        yp, conv_p = _ffn(x1, mod_p, norm2_g[l], wup, conv_w[l], conv_b[l], wdn, modf[:B], normf_g,
                          None, T=Tf, FC=256, final=final)
        outs_p.append((ret_p, k_p.reshape(B, WINDOW, ATT_KV_HEADS, ATT_HD),
                       v_p.reshape(B, WINDOW, ATT_KV_HEADS, ATT_HD), conv_p))

        hist = (cache_k[l].reshape(Bs, WINDOW, KV_WIDTH), cache_v[l].reshape(Bs, WINDOW, KV_WIDTH),
                state_ret[l])
        x1s, ret_s, k_s, v_s = _mixer(ys, mod_s, norm1_g[l], win, wout, attn_sinks[l], hist,
                                      T=Ls, RC=Ls, QG=Ls)
        ys, conv_s = _ffn(x1s, mod_s, norm2_g[l], wup, conv_w[l], conv_b[l], wdn, modf[B:], normf_g,
                          state_conv[l], T=Ls, FC=256, final=final)
        outs_s.append((ret_s, k_s.reshape(Bs, Ls, ATT_KV_HEADS, ATT_HD),
                       v_s.reshape(Bs, Ls, ATT_KV_HEADS, ATT_HD), conv_s))

    stack = lambda lst, i: jnp.stack([t[i] for t in lst])
    return (yp, ys,
            stack(outs_p, 0), stack(outs_p, 1), stack(outs_p, 2), stack(outs_p, 3),
            stack(outs_s, 0), stack(outs_s, 1), stack(outs_s, 2), stack(outs_s, 3))
```

```python
import functools

import jax
import jax.numpy as jnp
from jax import lax
from jax.experimental import pallas as pl
from jax.experimental.pallas import tpu as pltpu

F32 = jnp.float32
BF16 = jnp.bfloat16

CHUNK = 64
RET_HEADS = 4
RET_DK = 128
RET_DV = 128
RET_WIDTH = RET_HEADS * RET_DV
ATT_HEADS = 8
ATT_KV_HEADS = 2
ATT_HD = 64
ATT_REP = ATT_HEADS // ATT_KV_HEADS
ATT_WIDTH = ATT_HEADS * ATT_HD
KV_WIDTH = ATT_KV_HEADS * ATT_HD
WINDOW = 128
CONV_W = 3
EPS = 1e-6
CONV_PAD = 8
VMEM_LIMIT = 56 * 1024 * 1024
RET_LEAD = 4
ATT_LEAD = 2
ROWS_PER_STEP = 512

_OFF_RQ = 0
_OFF_RK = _OFF_RQ + RET_HEADS * RET_DK
_OFF_RV = _OFF_RK + RET_HEADS * RET_DK
_OFF_RG = _OFF_RV + RET_WIDTH
_OFF_AQ = _OFF_RG + RET_WIDTH
_OFF_AK = _OFF_AQ + ATT_WIDTH
_OFF_END = _OFF_AK + 2 * KV_WIDTH

_NT = (((1,), (1,)), ((), ()))
_TN = (((0,), (0,)), ((), ()))


def _silu(v):
    return v / (1.0 + jnp.exp(-v))


def _rms(v):
    return v * lax.rsqrt(jnp.mean(v * v, axis=-1, keepdims=True) + EPS)


def _rows_concat(parts):
    return parts[0] if len(parts) == 1 else jnp.concatenate(parts, axis=0)


def _ada_kernel(c_ref, w_ref, b_ref, o_ref):
    s = _silu(c_ref[...]).astype(BF16)
    o_ref[...] = jnp.dot(s, w_ref[...].astype(BF16), preferred_element_type=F32) + b_ref[...]


def _ada(c, w, b, tn=1024):
    m, d = c.shape
    n = w.shape[1]
    return pl.pallas_call(
        _ada_kernel,
        grid=(n // tn,),
        in_specs=[pl.BlockSpec((m, d), lambda i: (0, 0)),
                  pl.BlockSpec((d, tn), lambda i: (0, i)),
                  pl.BlockSpec((1, tn), lambda i: (0, i))],
        out_specs=pl.BlockSpec((m, tn), lambda i: (0, i)),
        out_shape=jax.ShapeDtypeStruct((m, n), F32),
        name="ada",
    )(c, w, b.reshape(1, n))


def _mixer_kernel(*refs, has_hist, BB, T, RC, QG):
    if has_hist:
        (x_ref, mod_ref, ln_ref, win_ref, wout_ref, sink_ref, bias_ref, dmat_ref, qdec_ref, kdec_ref,
         gl_ref, ck_ref, cv_ref, st_ref,
         x1_ref, ret_ref, krow_ref, vrow_ref,
         state_s, kbuf, vbuf, mixbuf) = refs
    else:
        (x_ref, mod_ref, ln_ref, win_ref, wout_ref, sink_ref, bias_ref, dmat_ref, qdec_ref, kdec_ref,
         gl_ref,
         x1_ref, ret_ref, krow_ref, vrow_ref,
         state_s, kbuf, vbuf, mixbuf) = refs
    j = pl.program_id(1)
    H = WINDOW
    W = krow_ref.shape[1]

    @pl.when(j == 0)
    def _():
        if has_hist:
            state_s[...] = st_ref[...]
            kbuf[:, 0:H, :] = ck_ref[...].astype(BF16)
            vbuf[:, 0:H, :] = cv_ref[...].astype(BF16)
        else:
            state_s[...] = jnp.zeros(state_s.shape, F32)
            kbuf[:, 0:H, :] = jnp.zeros((BB, H, KV_WIDTH), BF16)
            vbuf[:, 0:H, :] = jnp.zeros((BB, H, KV_WIDTH), BF16)

    hb = []
    for bb in range(BB):
        h = _rms(x_ref[bb]) * ln_ref[...]
        hb.append((h * (1.0 + mod_ref[bb, 1:2, :]) + mod_ref[bb, 0:1, :]).astype(BF16))
    proj = jnp.dot(_rows_concat(hb), win_ref[...], preferred_element_type=F32)
    rq = proj[:, _OFF_RQ:_OFF_RK].astype(BF16)
    rk = proj[:, _OFF_RK:_OFF_RV]
    rv = proj[:, _OFF_RV:_OFF_RG].astype(BF16)

    def ret_scores(bb, c, hh):
        rows = slice(bb * T + c * RC, bb * T + (c + 1) * RC)
        cs = slice(hh * RET_DK, (hh + 1) * RET_DK)
        q = rq[rows, cs]
        kf = rk[rows, cs]
        v = rv[rows, cs]
        s = lax.dot_general(q, kf.astype(BF16), _NT, preferred_element_type=F32)
        st = state_s[bb, hh]
        cross = jnp.dot(q, st.astype(BF16), preferred_element_type=F32)
        kd = (kf * kdec_ref[hh]).astype(BF16)
        state_s[bb, hh] = st * gl_ref[hh] + lax.dot_general(kd, v, _TN, preferred_element_type=F32)
        return rows, cs, hh, s, cross, v

    def ret_finish(ctx):
        rows, cs, hh, s, cross, v = ctx
        o = (jnp.dot((s * dmat_ref[hh]).astype(BF16), v, preferred_element_type=F32)
             + cross * qdec_ref[hh])
        mixbuf[rows, cs] = (_silu(rg[rows, cs]) * _rms(o)).astype(BF16)

    def att_scores(bb, p, g):
        q0 = bb * T + p * QG
        q4 = jnp.concatenate(
            [aq[q0:q0 + QG, (g * ATT_REP + r) * ATT_HD:(g * ATT_REP + r + 1) * ATT_HD]
             for r in range(ATT_REP)], axis=0)
        kb = kbuf[bb, p * QG:p * QG + QG + H, g * ATT_HD:(g + 1) * ATT_HD]
        vb = vbuf[bb, p * QG:p * QG + QG + H, g * ATT_HD:(g + 1) * ATT_HD]
        s = lax.dot_general(kb, q4, _NT, preferred_element_type=F32)
        return q0, p, g, s, vb

    def att_finish(ctx):
        q0, p, g, s, vb = ctx
        if (not has_hist) and p == 0:
            bias = bias_ref[g + ATT_KV_HEADS * (j == 0).astype(jnp.int32)]
        else:
            bias = bias_ref[g]
        s = s + bias
        sk = sink_ref[g]
        m = jnp.maximum(jnp.max(s, axis=0, keepdims=True), sk)
        pe = jnp.exp(s - m)
        den = jnp.sum(pe, axis=0, keepdims=True) + jnp.exp(sk - m)
        pn = (pe * (1.0 / den)).astype(BF16)
        o = lax.dot_general(pn, vb, _TN, preferred_element_type=F32)
        for r in range(ATT_REP):
            c0 = RET_WIDTH + (g * ATT_REP + r) * ATT_HD
            mixbuf[q0:q0 + QG, c0:c0 + ATT_HD] = o[r * QG:(r + 1) * QG, :].astype(BF16)

    ret_tasks = [(bb, c, hh) for c in range(T // RC) for bb in range(BB) for hh in range(RET_HEADS)]
    att_tasks = [(bb, p, g) for bb in range(BB) for p in range(T // QG) for g in range(ATT_KV_HEADS)]
    ret_ctx = [ret_scores(*t) for t in ret_tasks[:RET_LEAD]]
    rg = proj[:, _OFF_RG:_OFF_AQ]
    aq = (proj[:, _OFF_AQ:_OFF_AK] * (ATT_HD ** -0.5)).astype(BF16)
    ak = proj[:, _OFF_AK:_OFF_AK + KV_WIDTH]
    av = proj[:, _OFF_AK + KV_WIDTH:_OFF_END]
    for bb in range(BB):
        rows = slice(bb * T, (bb + 1) * T)
        kbuf[bb, H:H + T, :] = ak[rows, :].astype(BF16)
        vbuf[bb, H:H + T, :] = av[rows, :].astype(BF16)
        krow_ref[bb] = ak[(bb + 1) * T - W:(bb + 1) * T, :]
        vrow_ref[bb] = av[(bb + 1) * T - W:(bb + 1) * T, :]
    att_ctx = []
    for i in range(len(ret_tasks)):
        if len(ret_ctx) < len(ret_tasks):
            ret_ctx.append(ret_scores(*ret_tasks[len(ret_ctx)]))
        elif len(att_ctx) < min(ATT_LEAD, len(att_tasks)):
            att_ctx.append(att_scores(*att_tasks[len(att_ctx)]))
        ret_finish(ret_ctx[i])
        ret_ctx[i] = None
    for i in range(len(att_tasks)):
        if len(att_ctx) < len(att_tasks):
            att_ctx.append(att_scores(*att_tasks[len(att_ctx)]))
        att_finish(att_ctx[i])
        att_ctx[i] = None

    mix = jnp.dot(mixbuf[...], wout_ref[...], preferred_element_type=F32)
    for bb in range(BB):
        x1_ref[bb] = x_ref[bb] + mod_ref[bb, 2:3, :] * mix[bb * T:(bb + 1) * T, :]

    ret_ref[...] = state_s[...]
    if T >= H:
        kbuf[:, 0:H, :] = kbuf[:, T:T + H, :]
        vbuf[:, 0:H, :] = vbuf[:, T:T + H, :]


def _ret_log_gamma():
    return jnp.log1p(-jnp.exp2(-5.0 - jnp.arange(RET_HEADS, dtype=F32)))


def _retention_consts(RC):
    lg = _ret_log_gamma()
    i = jnp.arange(RC, dtype=F32)
    diff = i[:, None] - i[None, :]
    scale = RET_DK ** -0.5
    dmat = jnp.where(diff >= 0, jnp.exp(lg[:, None, None] * jnp.maximum(diff, 0.0)), 0.0) * scale
    qdec = jnp.exp(lg[:, None] * (i[None, :] + 1.0))
    kdec = jnp.exp(lg[:, None] * (RC - 1.0 - i[None, :])) * scale
    gl = jnp.exp(lg * RC)
    bc = lambda t: jnp.broadcast_to(t[:, :, None], (RET_HEADS, t.shape[1], RET_DV))
    return dmat, bc(qdec), bc(kdec), jnp.broadcast_to(gl[:, None, None], (RET_HEADS, 1, RET_DV))


def _attention_bias(QG, with_first):
    KB = QG + WINDOW
    qi = jnp.arange(QG)[:, None]
    ki = jnp.arange(KB)[None, :] - WINDOW
    qc = qi // CHUNK
    kc = jnp.floor_divide(ki, CHUNK)
    vis = (kc <= qc) & (kc >= qc - WINDOW // CHUNK)
    dist = jnp.abs(qi - ki).astype(F32)
    slopes = jnp.exp2(-8.0 * jnp.arange(1, ATT_HEADS + 1, dtype=F32) / ATT_HEADS)
    b = -slopes[:, None, None] * dist[None]
    b = jnp.where(vis[None], b, -jnp.inf)
    lay = lambda t: t.reshape(ATT_KV_HEADS, ATT_REP * QG, KB).swapaxes(1, 2)
    out = lay(b)
    if with_first:
        out = jnp.concatenate([out, lay(jnp.where((ki >= 0)[None], b, -jnp.inf))], axis=0)
    return out


def _const_spec(shape):
    nd = len(shape)
    return pl.BlockSpec(shape, lambda b, j: (0,) * nd, pipeline_mode=pl.Buffered(1))


def _seq_spec(BB, *tail):
    return pl.BlockSpec((BB,) + tail, lambda b, j: (b,) + (0,) * len(tail))


def _mixer(x, mod, ln, w_in, w_out, sinks, hist, BB, T, RC, QG):
    B, S, D = x.shape
    has_hist = hist is not None
    H = WINDOW
    dmat, qdec, kdec, gl = _retention_consts(RC)
    bias = _attention_bias(QG, not has_hist)
    sinkrow = jnp.repeat(sinks.astype(F32).reshape(ATT_KV_HEADS, ATT_REP, 1), QG, axis=2)
    sinkrow = sinkrow.reshape(ATT_KV_HEADS, 1, ATT_REP * QG)
    W = T if has_hist else H

    x_spec = pl.BlockSpec((BB, T, D), lambda b, j: (b, j, 0))
    in_specs = [
        x_spec,
        _seq_spec(BB, 6, D),
        _const_spec((1, D)),
        _const_spec(w_in.shape),
        _const_spec(w_out.shape),
        _const_spec(sinkrow.shape),
        _const_spec(bias.shape),
        _const_spec(dmat.shape),
        _const_spec(qdec.shape),
        _const_spec(kdec.shape),
        _const_spec(gl.shape),
    ]
    args = [x, mod, ln.reshape(1, D), w_in, w_out, sinkrow, bias, dmat, qdec, kdec, gl]
    if has_hist:
        in_specs += [_seq_spec(BB, H, KV_WIDTH), _seq_spec(BB, H, KV_WIDTH),
                     _seq_spec(BB, RET_HEADS, RET_DK, RET_DV)]
        args += list(hist)
    return pl.pallas_call(
        functools.partial(_mixer_kernel, has_hist=has_hist, BB=BB, T=T, RC=RC, QG=QG),
        grid=(B // BB, S // T),
        in_specs=in_specs,
        out_specs=[x_spec, _seq_spec(BB, RET_HEADS, RET_DK, RET_DV),
                   _seq_spec(BB, W, KV_WIDTH), _seq_spec(BB, W, KV_WIDTH)],
        out_shape=[jax.ShapeDtypeStruct((B, S, D), F32),
                   jax.ShapeDtypeStruct((B, RET_HEADS, RET_DK, RET_DV), F32),
                   jax.ShapeDtypeStruct((B, W, KV_WIDTH), F32),
                   jax.ShapeDtypeStruct((B, W, KV_WIDTH), F32)],
        scratch_shapes=[pltpu.VMEM((BB, RET_HEADS, RET_DK, RET_DV), F32),
                        pltpu.VMEM((BB, H + T, KV_WIDTH), BF16),
                        pltpu.VMEM((BB, H + T, KV_WIDTH), BF16),
                        pltpu.VMEM((BB * T, D), BF16)],
        compiler_params=pltpu.CompilerParams(
            dimension_semantics=("parallel", "arbitrary"), vmem_limit_bytes=VMEM_LIMIT),
        name="mixer_sample" if has_hist else "mixer_prompt",
    )(*args)


def _ffn_kernel(*refs, has_hist, BB, T, final):
    if has_hist:
        (x_ref, mod_ref, ln_ref, wup_ref, cw_ref, cb_ref, wdn_ref, modf_ref, lnf_ref, cs_ref,
         y_ref, cso_ref, prev_s) = refs
    else:
        (x_ref, mod_ref, ln_ref, wup_ref, cw_ref, cb_ref, wdn_ref, modf_ref, lnf_ref,
         y_ref, cso_ref, prev_s) = refs
    j = pl.program_id(1)
    F = cw_ref.shape[-1]

    @pl.when(j == 0)
    def _():
        if has_hist:
            prev_s[...] = cs_ref[...]
        else:
            prev_s[...] = jnp.zeros(prev_s.shape, F32)

    def shifted(a, k, tail):
        r = pltpu.roll(a, k, 0)
        row = lax.broadcasted_iota(jnp.int32, tail.shape, 0)
        head = jnp.where(row < k, pltpu.roll(tail, k, 0), r[0:CONV_PAD, :])
        return jnp.concatenate([head, r[CONV_PAD:, :]], axis=0)

    hb = []
    for bb in range(BB):
        h = _rms(x_ref[bb]) * ln_ref[...]
        hb.append((h * (1.0 + mod_ref[bb, 4:5, :]) + mod_ref[bb, 3:4, :]).astype(BF16))
    hb = _rows_concat(hb)
    a = jnp.dot(hb, wup_ref[:, :F], preferred_element_type=F32)
    b = jnp.dot(hb, wup_ref[:, F:], preferred_element_type=F32)
    conv = []
    for bb in range(BB):
        a_bb = a[bb * T:(bb + 1) * T, :]
        tail = prev_s[bb]
        conv.append(cb_ref[...] + (shifted(a_bb, 2, tail) * cw_ref[0:1, :] + shifted(a_bb, 1, tail) * cw_ref[1:2, :]
                                   + a_bb * cw_ref[2:3, :]))
        prev_s[bb] = a_bb[T - CONV_PAD:T, :]
    gate = (_silu(_rows_concat(conv)) * b).astype(BF16)
    ffn = jnp.dot(gate, wdn_ref[...], preferred_element_type=F32)
    for bb in range(BB):
        x2 = x_ref[bb] + mod_ref[bb, 5:6, :] * ffn[bb * T:(bb + 1) * T, :]
        if final:
            x2 = (_rms(x2) * lnf_ref[...]) * (1.0 + modf_ref[bb, 1:2, :]) + modf_ref[bb, 0:1, :]
        y_ref[bb] = x2

    cso_ref[...] = prev_s[...]


def _ffn(x1, mod, ln, w_up, conv_w, conv_b, w_down, modf, lnf, conv_state, BB, T, final):
    B, S, D = x1.shape
    F = conv_w.shape[-1]
    has_hist = conv_state is not None
    x_spec = pl.BlockSpec((BB, T, D), lambda b, j: (b, j, 0))
    in_specs = [
        x_spec,
        _seq_spec(BB, 6, D),
        _const_spec((1, D)),
        _const_spec(w_up.shape),
        _const_spec(conv_w.shape),
        _const_spec((1, F)),
        _const_spec(w_down.shape),
        _seq_spec(BB, 2, D),
        _const_spec((1, D)),
    ]
    args = [x1, mod, ln.reshape(1, D), w_up, conv_w, conv_b.reshape(1, F), w_down, modf, lnf.reshape(1, D)]
    if has_hist:
        in_specs.append(_seq_spec(BB, CONV_PAD, F))
        args.append(jnp.pad(conv_state, ((0, 0), (CONV_PAD - (CONV_W - 1), 0), (0, 0))))
    y, cso = pl.pallas_call(
        functools.partial(_ffn_kernel, has_hist=has_hist, BB=BB, T=T, final=final),
        grid=(B // BB, S // T),
        in_specs=in_specs,
        out_specs=[x_spec, _seq_spec(BB, CONV_PAD, F)],
        out_shape=[jax.ShapeDtypeStruct((B, S, D), F32),
                   jax.ShapeDtypeStruct((B, CONV_PAD, F), F32)],
        scratch_shapes=[pltpu.VMEM((BB, CONV_PAD, F), F32)],
        compiler_params=pltpu.CompilerParams(
            dimension_semantics=("parallel", "arbitrary"), vmem_limit_bytes=VMEM_LIMIT),
        name="ffn_sample" if has_hist else "ffn_prompt",
    )(*args)
    return y, cso[:, CONV_PAD - (CONV_W - 1):, :]


def _pick_tile(S, pref):
    t = min(S, pref)
    while S % t:
        t //= 2
    return t


def kernel(x_prompt, x_sample, cache_k, cache_v, state_ret, state_conv, c_prompt, c_sample, norm1_g, norm2_g, w_ada, b_ada, w_in, w_out, attn_sinks, w_up, conv_w, conv_b, w_down, normf_g, w_ada_f, b_ada_f):
    depth = w_in.shape[0]
    B, S, D = x_prompt.shape
    Bs, Ls, _ = x_sample.shape
    assert S % WINDOW == 0 and Ls == CHUNK and cache_k.shape[2] == WINDOW

    c_all = jnp.concatenate([c_prompt, c_sample], axis=0)
    modf = _ada(c_all, w_ada_f, b_ada_f).reshape(B + Bs, 2, D)
    Tm = _pick_tile(S, ROWS_PER_STEP)
    Tf = _pick_tile(S, 2 * ROWS_PER_STEP)
    RCp = _pick_tile(Tm, 256)
    BBs = _pick_tile(Bs, ROWS_PER_STEP // Ls)

    yp, ys = x_prompt, x_sample
    outs_p, outs_s = [], []
    for l in range(depth):
        mod = _ada(c_all, w_ada[l], b_ada[l]).reshape(B + Bs, 6, D)
        mod_p, mod_s = mod[:B], mod[B:]
        win = w_in[l].astype(BF16)
        wout = w_out[l].astype(BF16)
        wup = w_up[l].astype(BF16)
        wdn = w_down[l].astype(BF16)
        final = l == depth - 1

        x1, ret_p, k_p, v_p = _mixer(yp, mod_p, norm1_g[l], win, wout, attn_sinks[l], None,
                                     BB=1, T=Tm, RC=RCp, QG=WINDOW)
        yp, conv_p = _ffn(x1, mod_p, norm2_g[l], wup, conv_w[l], conv_b[l], wdn, modf[:B], normf_g,
                          None, BB=1, T=Tf, final=final)
        outs_p.append((ret_p, k_p.reshape(B, WINDOW, ATT_KV_HEADS, ATT_HD),
                       v_p.reshape(B, WINDOW, ATT_KV_HEADS, ATT_HD), conv_p))

        hist = (cache_k[l].reshape(Bs, WINDOW, KV_WIDTH), cache_v[l].reshape(Bs, WINDOW, KV_WIDTH),
                state_ret[l])
        x1s, ret_s, k_s, v_s = _mixer(ys, mod_s, norm1_g[l], win, wout, attn_sinks[l], hist,
                                      BB=BBs, T=Ls, RC=Ls, QG=Ls)
        ys, conv_s = _ffn(x1s, mod_s, norm2_g[l], wup, conv_w[l], conv_b[l], wdn, modf[B:], normf_g,
                          state_conv[l], BB=BBs, T=Ls, final=final)
        outs_s.append((ret_s, k_s.reshape(Bs, Ls, ATT_KV_HEADS, ATT_HD),
                       v_s.reshape(Bs, Ls, ATT_KV_HEADS, ATT_HD), conv_s))

    stack = lambda lst, i: jnp.stack([t[i] for t in lst])
    return (yp, ys,
            stack(outs_p, 0), stack(outs_p, 1), stack(outs_p, 2), stack(outs_p, 3),
            stack(outs_s, 0), stack(outs_s, 1), stack(outs_s, 2), stack(outs_s, 3))
```

```python
import functools

import jax
import jax.numpy as jnp
from jax import lax
from jax.experimental import pallas as pl
from jax.experimental.pallas import tpu as pltpu

F32 = jnp.float32
BF16 = jnp.bfloat16

CHUNK = 64
RET_HEADS = 4
RET_DK = 128
RET_DV = 128
RET_WIDTH = RET_HEADS * RET_DV
ATT_HEADS = 8
ATT_KV_HEADS = 2
ATT_HD = 64
ATT_REP = ATT_HEADS // ATT_KV_HEADS
ATT_WIDTH = ATT_HEADS * ATT_HD
KV_WIDTH = ATT_KV_HEADS * ATT_HD
WINDOW = 128
CONV_W = 3
EPS = 1e-6
CONV_PAD = 8
VMEM_LIMIT = 56 * 1024 * 1024
RET_LEAD = 4
ATT_LEAD = 2
ROWS_PER_STEP = 512

_OFF_RQ = 0
_OFF_RK = _OFF_RQ + RET_HEADS * RET_DK
_OFF_RV = _OFF_RK + RET_HEADS * RET_DK
_OFF_RG = _OFF_RV + RET_WIDTH
_OFF_AQ = _OFF_RG + RET_WIDTH
_OFF_AK = _OFF_AQ + ATT_WIDTH
_OFF_END = _OFF_AK + 2 * KV_WIDTH

_NT = (((1,), (1,)), ((), ()))
_TN = (((0,), (0,)), ((), ()))


def _silu(v):
    return v / (1.0 + jnp.exp(-v))


def _rms(v):
    return v * lax.rsqrt(jnp.mean(v * v, axis=-1, keepdims=True) + EPS)


def _rows_concat(parts):
    return parts[0] if len(parts) == 1 else jnp.concatenate(parts, axis=0)


def _ada_kernel(c_ref, w_ref, b_ref, o_ref):
    s = _silu(c_ref[...]).astype(BF16)
    o_ref[...] = jnp.dot(s, w_ref[...].astype(BF16), preferred_element_type=F32) + b_ref[...]


def _ada(c, w, b, tn=1024):
    m, d = c.shape
    n = w.shape[1]
    return pl.pallas_call(
        _ada_kernel,
        grid=(n // tn,),
        in_specs=[pl.BlockSpec((m, d), lambda i: (0, 0)),
                  pl.BlockSpec((d, tn), lambda i: (0, i)),
                  pl.BlockSpec((1, tn), lambda i: (0, i))],
        out_specs=pl.BlockSpec((m, tn), lambda i: (0, i)),
        out_shape=jax.ShapeDtypeStruct((m, n), F32),
        name="ada",
    )(c, w, b.reshape(1, n))


def _mixer_kernel(*refs, has_hist, BB, T, RC, QG):
    if has_hist:
        (x_ref, mod_ref, ln_ref, win_ref, wout_ref, sink_ref, bias_ref, dmat_ref, qdec_ref, kdec_ref,
         gl_ref, ck_ref, cv_ref, st_ref,
         x1_ref, ret_ref, krow_ref, vrow_ref,
         state_s, kbuf, vbuf, mixbuf) = refs
    else:
        (x_ref, mod_ref, ln_ref, win_ref, wout_ref, sink_ref, bias_ref, dmat_ref, qdec_ref, kdec_ref,
         gl_ref,
         x1_ref, ret_ref, krow_ref, vrow_ref,
         state_s, kbuf, vbuf, mixbuf) = refs
    j = pl.program_id(1)
    H = WINDOW
    W = krow_ref.shape[1]

    @pl.when(j == 0)
    def _():
        if has_hist:
            state_s[...] = st_ref[...]
            kbuf[:, 0:H, :] = ck_ref[...].astype(BF16)
            vbuf[:, 0:H, :] = cv_ref[...].astype(BF16)
        else:
            state_s[...] = jnp.zeros(state_s.shape, F32)
            kbuf[:, 0:H, :] = jnp.zeros((BB, H, KV_WIDTH), BF16)
            vbuf[:, 0:H, :] = jnp.zeros((BB, H, KV_WIDTH), BF16)

    hb = []
    for bb in range(BB):
        h = _rms(x_ref[bb]) * ln_ref[...]
        hb.append((h * (1.0 + mod_ref[bb, 1:2, :]) + mod_ref[bb, 0:1, :]).astype(BF16))
    proj = jnp.dot(_rows_concat(hb), win_ref[...], preferred_element_type=F32)
    rq = proj[:, _OFF_RQ:_OFF_RK].astype(BF16)
    rk = proj[:, _OFF_RK:_OFF_RV]
    rv = proj[:, _OFF_RV:_OFF_RG].astype(BF16)

    def ret_scores(bb, c, hh):
        rows = slice(bb * T + c * RC, bb * T + (c + 1) * RC)
        cs = slice(hh * RET_DK, (hh + 1) * RET_DK)
        q = rq[rows, cs]
        kf = rk[rows, cs]
        v = rv[rows, cs]
        s = lax.dot_general(q, kf.astype(BF16), _NT, preferred_element_type=F32)
        st = state_s[bb, hh]
        cross = jnp.dot(q, st.astype(BF16), preferred_element_type=F32)
        kd = (kf * kdec_ref[hh]).astype(BF16)
        state_s[bb, hh] = st * gl_ref[hh] + lax.dot_general(kd, v, _TN, preferred_element_type=F32)
        return rows, cs, hh, s, cross, v

    def ret_finish(ctx):
        rows, cs, hh, s, cross, v = ctx
        o = (jnp.dot((s * dmat_ref[hh]).astype(BF16), v, preferred_element_type=F32)
             + cross * qdec_ref[hh])
        mixbuf[rows, cs] = (_silu(rg[rows, cs]) * _rms(o)).astype(BF16)

    def att_scores(bb, p, g):
        q0 = bb * T + p * QG
        q4 = jnp.concatenate(
            [aq[q0:q0 + QG, (g * ATT_REP + r) * ATT_HD:(g * ATT_REP + r + 1) * ATT_HD]
             for r in range(ATT_REP)], axis=0)
        kb = kbuf[bb, p * QG:p * QG + QG + H, g * ATT_HD:(g + 1) * ATT_HD]
        vb = vbuf[bb, p * QG:p * QG + QG + H, g * ATT_HD:(g + 1) * ATT_HD]
        s = lax.dot_general(kb, q4, _NT, preferred_element_type=F32)
        return q0, p, g, s, vb

    def att_finish(ctx):
        q0, p, g, s, vb = ctx
        if (not has_hist) and p == 0:
            bias = bias_ref[g + ATT_KV_HEADS * (j == 0).astype(jnp.int32)]
        else:
            bias = bias_ref[g]
        s = s + bias
        sk = sink_ref[g]
        m = jnp.maximum(jnp.max(s, axis=0, keepdims=True), sk)
        pe = jnp.exp(s - m)
        den = jnp.sum(pe, axis=0, keepdims=True) + jnp.exp(sk - m)
        pn = (pe * (1.0 / den)).astype(BF16)
        o = lax.dot_general(pn, vb, _TN, preferred_element_type=F32)
        for r in range(ATT_REP):
            c0 = RET_WIDTH + (g * ATT_REP + r) * ATT_HD
            mixbuf[q0:q0 + QG, c0:c0 + ATT_HD] = o[r * QG:(r + 1) * QG, :].astype(BF16)

    ret_tasks = [(bb, c, hh) for c in range(T // RC) for bb in range(BB) for hh in range(RET_HEADS)]
    att_tasks = [(bb, p, g) for bb in range(BB) for p in range(T // QG) for g in range(ATT_KV_HEADS)]
    ret_ctx = [ret_scores(*t) for t in ret_tasks[:RET_LEAD]]
    rg = proj[:, _OFF_RG:_OFF_AQ]
    aq = (proj[:, _OFF_AQ:_OFF_AK] * (ATT_HD ** -0.5)).astype(BF16)
    ak = proj[:, _OFF_AK:_OFF_AK + KV_WIDTH]
    av = proj[:, _OFF_AK + KV_WIDTH:_OFF_END]
    for bb in range(BB):
        rows = slice(bb * T, (bb + 1) * T)
        kbuf[bb, H:H + T, :] = ak[rows, :].astype(BF16)
        vbuf[bb, H:H + T, :] = av[rows, :].astype(BF16)
        krow_ref[bb] = ak[(bb + 1) * T - W:(bb + 1) * T, :]
        vrow_ref[bb] = av[(bb + 1) * T - W:(bb + 1) * T, :]
    att_ctx = []
    for i in range(len(ret_tasks)):
        if len(ret_ctx) < len(ret_tasks):
            ret_ctx.append(ret_scores(*ret_tasks[len(ret_ctx)]))
        elif len(att_ctx) < min(ATT_LEAD, len(att_tasks)):
            att_ctx.append(att_scores(*att_tasks[len(att_ctx)]))
        ret_finish(ret_ctx[i])
        ret_ctx[i] = None
    for i in range(len(att_tasks)):
        if len(att_ctx) < len(att_tasks):
            att_ctx.append(att_scores(*att_tasks[len(att_ctx)]))
        att_finish(att_ctx[i])
        att_ctx[i] = None

    mix = jnp.dot(mixbuf[...], wout_ref[...], preferred_element_type=F32)
    for bb in range(BB):
        x1_ref[bb] = x_ref[bb] + mod_ref[bb, 2:3, :] * mix[bb * T:(bb + 1) * T, :]

    ret_ref[...] = state_s[...]
    if T >= H:
        kbuf[:, 0:H, :] = kbuf[:, T:T + H, :]
        vbuf[:, 0:H, :] = vbuf[:, T:T + H, :]


def _ret_log_gamma():
    return jnp.log1p(-jnp.exp2(-5.0 - jnp.arange(RET_HEADS, dtype=F32)))


def _retention_consts(RC):
    lg = _ret_log_gamma()
    i = jnp.arange(RC, dtype=F32)
    diff = i[:, None] - i[None, :]
    scale = RET_DK ** -0.5
    dmat = jnp.where(diff >= 0, jnp.exp(lg[:, None, None] * jnp.maximum(diff, 0.0)), 0.0) * scale
    qdec = jnp.exp(lg[:, None] * (i[None, :] + 1.0))
    kdec = jnp.exp(lg[:, None] * (RC - 1.0 - i[None, :])) * scale
    gl = jnp.exp(lg * RC)
    bc = lambda t: jnp.broadcast_to(t[:, :, None], (RET_HEADS, t.shape[1], RET_DV))
    return dmat, bc(qdec), bc(kdec), jnp.broadcast_to(gl[:, None, None], (RET_HEADS, 1, RET_DV))


def _attention_bias(QG, with_first):
    KB = QG + WINDOW
    qi = jnp.arange(QG)[:, None]
    ki = jnp.arange(KB)[None, :] - WINDOW
    qc = qi // CHUNK
    kc = jnp.floor_divide(ki, CHUNK)
    vis = (kc <= qc) & (kc >= qc - WINDOW // CHUNK)
    dist = jnp.abs(qi - ki).astype(F32)
    slopes = jnp.exp2(-8.0 * jnp.arange(1, ATT_HEADS + 1, dtype=F32) / ATT_HEADS)
    b = -slopes[:, None, None] * dist[None]
    b = jnp.where(vis[None], b, -jnp.inf)
    lay = lambda t: t.reshape(ATT_KV_HEADS, ATT_REP * QG, KB).swapaxes(1, 2)
    out = lay(b)
    if with_first:
        out = jnp.concatenate([out, lay(jnp.where((ki >= 0)[None], b, -jnp.inf))], axis=0)
    return out


def _const_spec(shape):
    nd = len(shape)
    return pl.BlockSpec(shape, lambda b, j: (0,) * nd, pipeline_mode=pl.Buffered(1))


def _seq_spec(BB, *tail):
    return pl.BlockSpec((BB,) + tail, lambda b, j: (b,) + (0,) * len(tail))


def _mixer(x, mod, ln, w_in, w_out, sinks, hist, BB, T, RC, QG):
    B, S, D = x.shape
    has_hist = hist is not None
    H = WINDOW
    dmat, qdec, kdec, gl = _retention_consts(RC)
    bias = _attention_bias(QG, not has_hist)
    sinkrow = jnp.repeat(sinks.astype(F32).reshape(ATT_KV_HEADS, ATT_REP, 1), QG, axis=2)
    sinkrow = sinkrow.reshape(ATT_KV_HEADS, 1, ATT_REP * QG)
    W = T if has_hist else H

    x_spec = pl.BlockSpec((BB, T, D), lambda b, j: (b, j, 0))
    in_specs = [
        x_spec,
        _seq_spec(BB, 6, D),
        _const_spec((1, D)),
        _const_spec(w_in.shape),
        _const_spec(w_out.shape),
        _const_spec(sinkrow.shape),
        _const_spec(bias.shape),
        _const_spec(dmat.shape),
        _const_spec(qdec.shape),
        _const_spec(kdec.shape),
        _const_spec(gl.shape),
    ]
    args = [x, mod, ln.reshape(1, D), w_in, w_out, sinkrow, bias, dmat, qdec, kdec, gl]
    if has_hist:
        in_specs += [_seq_spec(BB, H, KV_WIDTH), _seq_spec(BB, H, KV_WIDTH),
                     _seq_spec(BB, RET_HEADS, RET_DK, RET_DV)]
        args += list(hist)
    return pl.pallas_call(
        functools.partial(_mixer_kernel, has_hist=has_hist, BB=BB, T=T, RC=RC, QG=QG),
        grid=(B // BB, S // T),
        in_specs=in_specs,
        out_specs=[x_spec, _seq_spec(BB, RET_HEADS, RET_DK, RET_DV),
                   _seq_spec(BB, W, KV_WIDTH), _seq_spec(BB, W, KV_WIDTH)],
        out_shape=[jax.ShapeDtypeStruct((B, S, D), F32),
                   jax.ShapeDtypeStruct((B, RET_HEADS, RET_DK, RET_DV), F32),
                   jax.ShapeDtypeStruct((B, W, KV_WIDTH), F32),
                   jax.ShapeDtypeStruct((B, W, KV_WIDTH), F32)],
        scratch_shapes=[pltpu.VMEM((BB, RET_HEADS, RET_DK, RET_DV), F32),
                        pltpu.VMEM((BB, H + T, KV_WIDTH), BF16),
                        pltpu.VMEM((BB, H + T, KV_WIDTH), BF16),
                        pltpu.VMEM((BB * T, D), BF16)],
        compiler_params=pltpu.CompilerParams(
            dimension_semantics=("parallel", "arbitrary"), vmem_limit_bytes=VMEM_LIMIT),
        name="mixer_sample" if has_hist else "mixer_prompt",
    )(*args)


def _ffn_kernel(*refs, has_hist, BB, T, NG, final):
    if has_hist:
        (x_ref, mod_ref, ln_ref, wup_ref, cw_ref, cb_ref, wdn_ref, modf_ref, lnf_ref, cs_ref,
         y_ref, cso_ref, prev_s) = refs
    else:
        (x_ref, mod_ref, ln_ref, wup_ref, cw_ref, cb_ref, wdn_ref, modf_ref, lnf_ref,
         y_ref, cso_ref, prev_s) = refs
    j = pl.program_id(1)
    F = cw_ref.shape[-1]

    @pl.when(j == 0)
    def _():
        if has_hist:
            prev_s[...] = cs_ref[...]
        else:
            prev_s[...] = jnp.zeros(prev_s.shape, F32)

    def shifted(a, k, tail):
        r = pltpu.roll(a, k, 0)
        row = lax.broadcasted_iota(jnp.int32, tail.shape, 0)
        head = jnp.where(row < k, pltpu.roll(tail, k, 0), r[0:CONV_PAD, :])
        return jnp.concatenate([head, r[CONV_PAD:, :]], axis=0)

    GR = (BB * T) // NG
    if BB == 1:
        groups = [[(0, g * GR, (g + 1) * GR)] for g in range(NG)]
    else:
        groups = [[(bb, 0, T) for bb in range(g * (GR // T), (g + 1) * (GR // T))] for g in range(NG)]
    up = []
    for segs in groups:
        hb = []
        for bb, r0, r1 in segs:
            h = _rms(x_ref[bb, r0:r1, :]) * ln_ref[...]
            hb.append((h * (1.0 + mod_ref[bb, 4:5, :]) + mod_ref[bb, 3:4, :]).astype(BF16))
        hb = _rows_concat(hb)
        up.append((jnp.dot(hb, wup_ref[:, :F], preferred_element_type=F32),
                   jnp.dot(hb, wup_ref[:, F:], preferred_element_type=F32)))
    tails = [prev_s[bb] for bb in range(BB)]
    ffn = []
    for segs, (a, b) in zip(groups, up):
        conv = []
        o = 0
        for bb, r0, r1 in segs:
            a_s = a[o:o + r1 - r0, :]
            conv.append(cb_ref[...] + (shifted(a_s, 2, tails[bb]) * cw_ref[0:1, :]
                                       + shifted(a_s, 1, tails[bb]) * cw_ref[1:2, :] + a_s * cw_ref[2:3, :]))
            tails[bb] = a_s[r1 - r0 - CONV_PAD:r1 - r0, :]
            o += r1 - r0
        gate = (_silu(_rows_concat(conv)) * b).astype(BF16)
        ffn.append(jnp.dot(gate, wdn_ref[...], preferred_element_type=F32))
    for bb in range(BB):
        prev_s[bb] = tails[bb]
    for segs, f in zip(groups, ffn):
        o = 0
        for bb, r0, r1 in segs:
            x2 = x_ref[bb, r0:r1, :] + mod_ref[bb, 5:6, :] * f[o:o + r1 - r0, :]
            if final:
                x2 = (_rms(x2) * lnf_ref[...]) * (1.0 + modf_ref[bb, 1:2, :]) + modf_ref[bb, 0:1, :]
            y_ref[bb, r0:r1, :] = x2
            o += r1 - r0

    cso_ref[...] = prev_s[...]


def _ffn(x1, mod, ln, w_up, conv_w, conv_b, w_down, modf, lnf, conv_state, BB, T, NG, final):
    B, S, D = x1.shape
    F = conv_w.shape[-1]
    has_hist = conv_state is not None
    x_spec = pl.BlockSpec((BB, T, D), lambda b, j: (b, j, 0))
    in_specs = [
        x_spec,
        _seq_spec(BB, 6, D),
        _const_spec((1, D)),
        _const_spec(w_up.shape),
        _const_spec(conv_w.shape),
        _const_spec((1, F)),
        _const_spec(w_down.shape),
        _seq_spec(BB, 2, D),
        _const_spec((1, D)),
    ]
    args = [x1, mod, ln.reshape(1, D), w_up, conv_w, conv_b.reshape(1, F), w_down, modf, lnf.reshape(1, D)]
    if has_hist:
        in_specs.append(_seq_spec(BB, CONV_PAD, F))
        args.append(jnp.pad(conv_state, ((0, 0), (CONV_PAD - (CONV_W - 1), 0), (0, 0))))
    y, cso = pl.pallas_call(
        functools.partial(_ffn_kernel, has_hist=has_hist, BB=BB, T=T, NG=NG, final=final),
        grid=(B // BB, S // T),
        in_specs=in_specs,
        out_specs=[x_spec, _seq_spec(BB, CONV_PAD, F)],
        out_shape=[jax.ShapeDtypeStruct((B, S, D), F32),
                   jax.ShapeDtypeStruct((B, CONV_PAD, F), F32)],
        scratch_shapes=[pltpu.VMEM((BB, CONV_PAD, F), F32)],
        compiler_params=pltpu.CompilerParams(
            dimension_semantics=("parallel", "arbitrary"), vmem_limit_bytes=VMEM_LIMIT),
        name="ffn_sample" if has_hist else "ffn_prompt",
    )(*args)
    return y, cso[:, CONV_PAD - (CONV_W - 1):, :]


def _pick_tile(S, pref):
    t = min(S, pref)
    while S % t:
        t //= 2
    return t


def kernel(x_prompt, x_sample, cache_k, cache_v, state_ret, state_conv, c_prompt, c_sample, norm1_g, norm2_g, w_ada, b_ada, w_in, w_out, attn_sinks, w_up, conv_w, conv_b, w_down, normf_g, w_ada_f, b_ada_f):
    depth = w_in.shape[0]
    B, S, D = x_prompt.shape
    Bs, Ls, _ = x_sample.shape
    assert S % WINDOW == 0 and Ls == CHUNK and cache_k.shape[2] == WINDOW

    c_all = jnp.concatenate([c_prompt, c_sample], axis=0)
    modf = _ada(c_all, w_ada_f, b_ada_f).reshape(B + Bs, 2, D)
    Tm = _pick_tile(S, 2 * ROWS_PER_STEP)
    Tf = _pick_tile(S, 2 * ROWS_PER_STEP)
    RCp = _pick_tile(Tm, 256)
    BBs = _pick_tile(Bs, ROWS_PER_STEP // Ls)

    yp, ys = x_prompt, x_sample
    outs_p, outs_s = [], []
    for l in range(depth):
        mod = _ada(c_all, w_ada[l], b_ada[l]).reshape(B + Bs, 6, D)
        mod_p, mod_s = mod[:B], mod[B:]
        win = w_in[l].astype(BF16)
        wout = w_out[l].astype(BF16)
        wup = w_up[l].astype(BF16)
        wdn = w_down[l].astype(BF16)
        final = l == depth - 1

        x1, ret_p, k_p, v_p = _mixer(yp, mod_p, norm1_g[l], win, wout, attn_sinks[l], None,
                                     BB=1, T=Tm, RC=RCp, QG=WINDOW)
        yp, conv_p = _ffn(x1, mod_p, norm2_g[l], wup, conv_w[l], conv_b[l], wdn, modf[:B], normf_g,
                          None, BB=1, T=Tf, NG=4, final=final)
        outs_p.append((ret_p, k_p.reshape(B, WINDOW, ATT_KV_HEADS, ATT_HD),
                       v_p.reshape(B, WINDOW, ATT_KV_HEADS, ATT_HD), conv_p))

        hist = (cache_k[l].reshape(Bs, WINDOW, KV_WIDTH), cache_v[l].reshape(Bs, WINDOW, KV_WIDTH),
                state_ret[l])
        x1s, ret_s, k_s, v_s = _mixer(ys, mod_s, norm1_g[l], win, wout, attn_sinks[l], hist,
                                      BB=BBs, T=Ls, RC=Ls, QG=Ls)
        ys, conv_s = _ffn(x1s, mod_s, norm2_g[l], wup, conv_w[l], conv_b[l], wdn, modf[B:], normf_g,
                          state_conv[l], BB=BBs, T=Ls, NG=1, final=final)
        outs_s.append((ret_s, k_s.reshape(Bs, Ls, ATT_KV_HEADS, ATT_HD),
                       v_s.reshape(Bs, Ls, ATT_KV_HEADS, ATT_HD), conv_s))

    stack = lambda lst, i: jnp.stack([t[i] for t in lst])
    return (yp, ys,
            stack(outs_p, 0), stack(outs_p, 1), stack(outs_p, 2), stack(outs_p, 3),
            stack(outs_s, 0), stack(outs_s, 1), stack(outs_s, 2), stack(outs_s, 3))
```

```python
import functools

import jax
import jax.numpy as jnp
from jax import lax
from jax.experimental import pallas as pl
from jax.experimental.pallas import tpu as pltpu

F32 = jnp.float32
BF16 = jnp.bfloat16

CHUNK = 64
RET_HEADS = 4
RET_DK = 128
RET_DV = 128
RET_WIDTH = RET_HEADS * RET_DV
ATT_HEADS = 8
ATT_KV_HEADS = 2
ATT_HD = 64
ATT_REP = ATT_HEADS // ATT_KV_HEADS
ATT_WIDTH = ATT_HEADS * ATT_HD
KV_WIDTH = ATT_KV_HEADS * ATT_HD
WINDOW = 128
CONV_W = 3
EPS = 1e-6
CONV_PAD = 8
VMEM_LIMIT = 56 * 1024 * 1024
RET_LEAD = 4
ATT_LEAD = 2
ROWS_PER_STEP = 512

_OFF_RQ = 0
_OFF_RK = _OFF_RQ + RET_HEADS * RET_DK
_OFF_RV = _OFF_RK + RET_HEADS * RET_DK
_OFF_RG = _OFF_RV + RET_WIDTH
_OFF_AQ = _OFF_RG + RET_WIDTH
_OFF_AK = _OFF_AQ + ATT_WIDTH
_OFF_END = _OFF_AK + 2 * KV_WIDTH

_NT = (((1,), (1,)), ((), ()))
_TN = (((0,), (0,)), ((), ()))


def _silu(v):
    return v / (1.0 + jnp.exp(-v))


def _rms(v):
    return v * lax.rsqrt(jnp.mean(v * v, axis=-1, keepdims=True) + EPS)


def _rows_concat(parts):
    return parts[0] if len(parts) == 1 else jnp.concatenate(parts, axis=0)


def _ada_kernel(c_ref, w_ref, b_ref, o_ref):
    s = _silu(c_ref[...]).astype(BF16)
    o_ref[...] = jnp.dot(s, w_ref[...].astype(BF16), preferred_element_type=F32) + b_ref[...]


def _ada(c, w, b, tn=1024):
    m, d = c.shape
    n = w.shape[1]
    return pl.pallas_call(
        _ada_kernel,
        grid=(n // tn,),
        in_specs=[pl.BlockSpec((m, d), lambda i: (0, 0)),
                  pl.BlockSpec((d, tn), lambda i: (0, i)),
                  pl.BlockSpec((1, tn), lambda i: (0, i))],
        out_specs=pl.BlockSpec((m, tn), lambda i: (0, i)),
        out_shape=jax.ShapeDtypeStruct((m, n), F32),
        name="ada",
    )(c, w, b.reshape(1, n))


def _mixer_kernel(*refs, has_hist, BB, T, RC, QG, NG):
    if has_hist:
        (x_ref, mod_ref, ln_ref, win_ref, wout_ref, sink_ref, bias_ref, dmat_ref, qdec_ref, kdec_ref,
         gl_ref, ck_ref, cv_ref, st_ref,
         x1_ref, ret_ref, krow_ref, vrow_ref,
         state_s, kbuf, vbuf, mixbuf) = refs
    else:
        (x_ref, mod_ref, ln_ref, win_ref, wout_ref, sink_ref, bias_ref, dmat_ref, qdec_ref, kdec_ref,
         gl_ref,
         x1_ref, ret_ref, krow_ref, vrow_ref,
         state_s, kbuf, vbuf, mixbuf) = refs
    j = pl.program_id(1)
    H = WINDOW
    W = krow_ref.shape[1]

    @pl.when(j == 0)
    def _():
        if has_hist:
            state_s[...] = st_ref[...]
            kbuf[:, 0:H, :] = ck_ref[...].astype(BF16)
            vbuf[:, 0:H, :] = cv_ref[...].astype(BF16)
        else:
            state_s[...] = jnp.zeros(state_s.shape, F32)
            kbuf[:, 0:H, :] = jnp.zeros((BB, H, KV_WIDTH), BF16)
            vbuf[:, 0:H, :] = jnp.zeros((BB, H, KV_WIDTH), BF16)

    GR = (BB * T) // NG
    SEG = min(GR, T)
    proj = []
    for g in range(NG):
        hb = []
        for r in range(g * GR, (g + 1) * GR, SEG):
            bb, r0 = divmod(r, T)
            h = _rms(x_ref[bb, r0:r0 + SEG, :]) * ln_ref[...]
            hb.append((h * (1.0 + mod_ref[bb, 1:2, :]) + mod_ref[bb, 0:1, :]).astype(BF16))
        proj.append(jnp.dot(_rows_concat(hb), win_ref[...], preferred_element_type=F32))
    proj = _rows_concat(proj)
    rq = proj[:, _OFF_RQ:_OFF_RK].astype(BF16)
    rk = proj[:, _OFF_RK:_OFF_RV]
    rv = proj[:, _OFF_RV:_OFF_RG].astype(BF16)

    def ret_scores(bb, c, hh):
        rows = slice(bb * T + c * RC, bb * T + (c + 1) * RC)
        cs = slice(hh * RET_DK, (hh + 1) * RET_DK)
        q = rq[rows, cs]
        kf = rk[rows, cs]
        v = rv[rows, cs]
        s = lax.dot_general(q, kf.astype(BF16), _NT, preferred_element_type=F32)
        st = state_s[bb, hh]
        cross = jnp.dot(q, st.astype(BF16), preferred_element_type=F32)
        kd = (kf * kdec_ref[hh]).astype(BF16)
        state_s[bb, hh] = st * gl_ref[hh] + lax.dot_general(kd, v, _TN, preferred_element_type=F32)
        return rows, cs, hh, s, cross, v

    def ret_finish(ctx):
        rows, cs, hh, s, cross, v = ctx
        o = (jnp.dot((s * dmat_ref[hh]).astype(BF16), v, preferred_element_type=F32)
             + cross * qdec_ref[hh])
        mixbuf[rows, cs] = (_silu(rg[rows, cs]) * _rms(o)).astype(BF16)

    def att_scores(bb, p):
        q0 = bb * T + p * QG
        ss = []
        for g in range(ATT_KV_HEADS):
            q4 = jnp.concatenate(
                [aq[q0:q0 + QG, (g * ATT_REP + r) * ATT_HD:(g * ATT_REP + r + 1) * ATT_HD]
                 for r in range(ATT_REP)], axis=0)
            kb = kbuf[bb, p * QG:p * QG + QG + H, g * ATT_HD:(g + 1) * ATT_HD]
            ss.append(lax.dot_general(kb, q4, _NT, preferred_element_type=F32))
        return bb, p, ss

    def att_finish(ctx):
        bb, p, ss = ctx
        q0 = bb * T + p * QG
        pn = []
        for g, s in enumerate(ss):
            if (not has_hist) and p == 0:
                bias = bias_ref[g + ATT_KV_HEADS * (j == 0).astype(jnp.int32)]
            else:
                bias = bias_ref[g]
            s = s + bias
            sk = sink_ref[g]
            m = jnp.maximum(jnp.max(s, axis=0, keepdims=True), sk)
            pe = jnp.exp(s - m)
            den = jnp.sum(pe, axis=0, keepdims=True) + jnp.exp(sk - m)
            pn.append((pe * (1.0 / den)).astype(BF16))
        o = lax.dot_general(jnp.concatenate(pn, axis=1), vbuf[bb, p * QG:p * QG + QG + H, :], _TN,
                            preferred_element_type=F32)
        for g in range(ATT_KV_HEADS):
            for r in range(ATT_REP):
                c0 = RET_WIDTH + (g * ATT_REP + r) * ATT_HD
                r0 = (g * ATT_REP + r) * QG
                mixbuf[q0:q0 + QG, c0:c0 + ATT_HD] = o[r0:r0 + QG, g * ATT_HD:(g + 1) * ATT_HD].astype(BF16)

    ret_tasks = [(bb, c, hh) for c in range(T // RC) for bb in range(BB) for hh in range(RET_HEADS)]
    att_tasks = [(bb, p) for bb in range(BB) for p in range(T // QG)]
    ret_ctx = [ret_scores(*t) for t in ret_tasks[:RET_LEAD]]
    rg = proj[:, _OFF_RG:_OFF_AQ]
    aq = (proj[:, _OFF_AQ:_OFF_AK] * (ATT_HD ** -0.5)).astype(BF16)
    ak = proj[:, _OFF_AK:_OFF_AK + KV_WIDTH]
    av = proj[:, _OFF_AK + KV_WIDTH:_OFF_END]
    for bb in range(BB):
        rows = slice(bb * T, (bb + 1) * T)
        kbuf[bb, H:H + T, :] = ak[rows, :].astype(BF16)
        vbuf[bb, H:H + T, :] = av[rows, :].astype(BF16)
        krow_ref[bb] = ak[(bb + 1) * T - W:(bb + 1) * T, :]
        vrow_ref[bb] = av[(bb + 1) * T - W:(bb + 1) * T, :]
    att_ctx = []
    for i in range(len(ret_tasks)):
        if len(ret_ctx) < len(ret_tasks):
            ret_ctx.append(ret_scores(*ret_tasks[len(ret_ctx)]))
        elif len(att_ctx) < min(ATT_LEAD, len(att_tasks)):
            att_ctx.append(att_scores(*att_tasks[len(att_ctx)]))
        ret_finish(ret_ctx[i])
        ret_ctx[i] = None
    def out_group(g):
        mix = jnp.dot(mixbuf[g * GR:(g + 1) * GR, :], wout_ref[...], preferred_element_type=F32)
        for r in range(g * GR, (g + 1) * GR, SEG):
            bb, r0 = divmod(r, T)
            x1_ref[bb, r0:r0 + SEG, :] = (x_ref[bb, r0:r0 + SEG, :]
                                          + mod_ref[bb, 2:3, :] * mix[r - g * GR:r - g * GR + SEG, :])

    done = 0
    for i in range(len(att_tasks)):
        if len(att_ctx) < len(att_tasks):
            att_ctx.append(att_scores(*att_tasks[len(att_ctx)]))
        att_finish(att_ctx[i])
        att_ctx[i] = None
        bb, p = att_tasks[i]
        rows_done = bb * T + (p + 1) * QG
        while (done + 1) * GR <= rows_done:
            out_group(done)
            done += 1
    assert done == NG

    ret_ref[...] = state_s[...]
    if T >= H:
        kbuf[:, 0:H, :] = kbuf[:, T:T + H, :]
        vbuf[:, 0:H, :] = vbuf[:, T:T + H, :]


def _ret_log_gamma():
    return jnp.log1p(-jnp.exp2(-5.0 - jnp.arange(RET_HEADS, dtype=F32)))


def _retention_consts(RC):
    lg = _ret_log_gamma()
    i = jnp.arange(RC, dtype=F32)
    diff = i[:, None] - i[None, :]
    scale = RET_DK ** -0.5
    dmat = jnp.where(diff >= 0, jnp.exp(lg[:, None, None] * jnp.maximum(diff, 0.0)), 0.0) * scale
    qdec = jnp.exp(lg[:, None] * (i[None, :] + 1.0))
    kdec = jnp.exp(lg[:, None] * (RC - 1.0 - i[None, :])) * scale
    gl = jnp.exp(lg * RC)
    bc = lambda t: jnp.broadcast_to(t[:, :, None], (RET_HEADS, t.shape[1], RET_DV))
    return dmat, bc(qdec), bc(kdec), jnp.broadcast_to(gl[:, None, None], (RET_HEADS, 1, RET_DV))


def _attention_bias(QG, with_first):
    KB = QG + WINDOW
    qi = jnp.arange(QG)[:, None]
    ki = jnp.arange(KB)[None, :] - WINDOW
    qc = qi // CHUNK
    kc = jnp.floor_divide(ki, CHUNK)
    vis = (kc <= qc) & (kc >= qc - WINDOW // CHUNK)
    dist = jnp.abs(qi - ki).astype(F32)
    slopes = jnp.exp2(-8.0 * jnp.arange(1, ATT_HEADS + 1, dtype=F32) / ATT_HEADS)
    b = -slopes[:, None, None] * dist[None]
    b = jnp.where(vis[None], b, -jnp.inf)
    lay = lambda t: t.reshape(ATT_KV_HEADS, ATT_REP * QG, KB).swapaxes(1, 2)
    out = lay(b)
    if with_first:
        out = jnp.concatenate([out, lay(jnp.where((ki >= 0)[None], b, -jnp.inf))], axis=0)
    return out


def _const_spec(shape):
    nd = len(shape)
    return pl.BlockSpec(shape, lambda b, j: (0,) * nd, pipeline_mode=pl.Buffered(1))


def _seq_spec(BB, *tail, first=0):
    assert first % BB == 0
    return pl.BlockSpec((BB,) + tail, lambda b, j: (b + first // BB,) + (0,) * len(tail))


def _mixer(x, mod, mod_first, ln, w_in, w_out, sinks, hist, BB, T, RC, QG, NG):
    B, S, D = x.shape
    has_hist = hist is not None
    H = WINDOW
    dmat, qdec, kdec, gl = _retention_consts(RC)
    bias = _attention_bias(QG, not has_hist)
    sinkrow = jnp.repeat(sinks.astype(F32).reshape(ATT_KV_HEADS, ATT_REP, 1), QG, axis=2)
    sinkrow = sinkrow.reshape(ATT_KV_HEADS, 1, ATT_REP * QG)
    W = T if has_hist else H

    x_spec = pl.BlockSpec((BB, T, D), lambda b, j: (b, j, 0))
    in_specs = [
        x_spec,
        _seq_spec(BB, 6, D, first=mod_first),
        _const_spec((1, D)),
        _const_spec(w_in.shape),
        _const_spec(w_out.shape),
        _const_spec(sinkrow.shape),
        _const_spec(bias.shape),
        _const_spec(dmat.shape),
        _const_spec(qdec.shape),
        _const_spec(kdec.shape),
        _const_spec(gl.shape),
    ]
    args = [x, mod, ln.reshape(1, D), w_in, w_out, sinkrow, bias, dmat, qdec, kdec, gl]
    if has_hist:
        in_specs += [_seq_spec(BB, H, KV_WIDTH), _seq_spec(BB, H, KV_WIDTH),
                     _seq_spec(BB, RET_HEADS, RET_DK, RET_DV)]
        args += list(hist)
    return pl.pallas_call(
        functools.partial(_mixer_kernel, has_hist=has_hist, BB=BB, T=T, RC=RC, QG=QG, NG=NG),
        grid=(B // BB, S // T),
        in_specs=in_specs,
        out_specs=[x_spec, _seq_spec(BB, RET_HEADS, RET_DK, RET_DV),
                   _seq_spec(BB, W, KV_WIDTH), _seq_spec(BB, W, KV_WIDTH)],
        out_shape=[jax.ShapeDtypeStruct((B, S, D), F32),
                   jax.ShapeDtypeStruct((B, RET_HEADS, RET_DK, RET_DV), F32),
                   jax.ShapeDtypeStruct((B, W, KV_WIDTH), F32),
                   jax.ShapeDtypeStruct((B, W, KV_WIDTH), F32)],
        scratch_shapes=[pltpu.VMEM((BB, RET_HEADS, RET_DK, RET_DV), F32),
                        pltpu.VMEM((BB, H + T, KV_WIDTH), BF16),
                        pltpu.VMEM((BB, H + T, KV_WIDTH), BF16),
                        pltpu.VMEM((BB * T, D), BF16)],
        compiler_params=pltpu.CompilerParams(
            dimension_semantics=("parallel", "arbitrary"), vmem_limit_bytes=VMEM_LIMIT),
        name="mixer_sample" if has_hist else "mixer_prompt",
    )(*args)


def _ffn_kernel(*refs, has_hist, BB, T, NG, final):
    if has_hist:
        (x_ref, mod_ref, ln_ref, wup_ref, cw_ref, cb_ref, wdn_ref, modf_ref, lnf_ref, cs_ref,
         y_ref, cso_ref, prev_s) = refs
    else:
        (x_ref, mod_ref, ln_ref, wup_ref, cw_ref, cb_ref, wdn_ref, modf_ref, lnf_ref,
         y_ref, cso_ref, prev_s) = refs
    j = pl.program_id(1)
    F = cw_ref.shape[-1]

    @pl.when(j == 0)
    def _():
        prev_s[...] = jnp.zeros(prev_s.shape, F32)
        if has_hist:
            prev_s[:, CONV_PAD - (CONV_W - 1):, :] = cs_ref[...]

    def shifted(a, k, tail):
        r = pltpu.roll(a, k, 0)
        row = lax.broadcasted_iota(jnp.int32, tail.shape, 0)
        head = jnp.where(row < k, pltpu.roll(tail, k, 0), r[0:CONV_PAD, :])
        return jnp.concatenate([head, r[CONV_PAD:, :]], axis=0)

    GR = (BB * T) // NG
    if BB == 1:
        groups = [[(0, g * GR, (g + 1) * GR)] for g in range(NG)]
    else:
        groups = [[(bb, 0, T) for bb in range(g * (GR // T), (g + 1) * (GR // T))] for g in range(NG)]
    up = []
    for segs in groups:
        hb = []
        for bb, r0, r1 in segs:
            h = _rms(x_ref[bb, r0:r1, :]) * ln_ref[...]
            hb.append((h * (1.0 + mod_ref[bb, 4:5, :]) + mod_ref[bb, 3:4, :]).astype(BF16))
        hb = _rows_concat(hb)
        up.append((jnp.dot(hb, wup_ref[:, :F], preferred_element_type=F32),
                   jnp.dot(hb, wup_ref[:, F:], preferred_element_type=F32)))
    def finish(segs, f):
        o = 0
        for bb, r0, r1 in segs:
            x2 = x_ref[bb, r0:r1, :] + mod_ref[bb, 5:6, :] * f[o:o + r1 - r0, :]
            if final:
                x2 = (_rms(x2) * lnf_ref[...]) * (1.0 + modf_ref[bb, 1:2, :]) + modf_ref[bb, 0:1, :]
            y_ref[bb, r0:r1, :] = x2
            o += r1 - r0

    tails = [prev_s[bb] for bb in range(BB)]
    ffn = []
    for segs, (a, b) in zip(groups, up):
        conv = []
        o = 0
        for bb, r0, r1 in segs:
            a_s = a[o:o + r1 - r0, :]
            conv.append(cb_ref[...] + (shifted(a_s, 2, tails[bb]) * cw_ref[0:1, :]
                                       + shifted(a_s, 1, tails[bb]) * cw_ref[1:2, :] + a_s * cw_ref[2:3, :]))
            tails[bb] = a_s[r1 - r0 - CONV_PAD:r1 - r0, :]
            o += r1 - r0
        gate = (_silu(_rows_concat(conv)) * b).astype(BF16)
        ffn.append(jnp.dot(gate, wdn_ref[...], preferred_element_type=F32))
    for bb in range(BB):
        prev_s[bb] = tails[bb]
    for segs, f in zip(groups, ffn):
        finish(segs, f)

    cso_ref[...] = prev_s[:, CONV_PAD - (CONV_W - 1):, :]


def _ffn(x1, mod, mod_first, ln, w_up, conv_w, conv_b, w_down, modf, lnf, conv_state, BB, T, NG, final):
    B, S, D = x1.shape
    F = conv_w.shape[-1]
    has_hist = conv_state is not None
    x_spec = pl.BlockSpec((BB, T, D), lambda b, j: (b, j, 0))
    in_specs = [
        x_spec,
        _seq_spec(BB, 6, D, first=mod_first),
        _const_spec((1, D)),
        _const_spec(w_up.shape),
        _const_spec(conv_w.shape),
        _const_spec((1, F)),
        _const_spec(w_down.shape),
        _seq_spec(BB, 2, D, first=mod_first),
        _const_spec((1, D)),
    ]
    args = [x1, mod, ln.reshape(1, D), w_up, conv_w, conv_b.reshape(1, F), w_down, modf, lnf.reshape(1, D)]
    if has_hist:
        in_specs.append(_seq_spec(BB, CONV_W - 1, F))
        args.append(conv_state)
    return pl.pallas_call(
        functools.partial(_ffn_kernel, has_hist=has_hist, BB=BB, T=T, NG=NG, final=final),
        grid=(B // BB, S // T),
        in_specs=in_specs,
        out_specs=[x_spec, _seq_spec(BB, CONV_W - 1, F)],
        out_shape=[jax.ShapeDtypeStruct((B, S, D), F32),
                   jax.ShapeDtypeStruct((B, CONV_W - 1, F), F32)],
        scratch_shapes=[pltpu.VMEM((BB, CONV_PAD, F), F32)],
        compiler_params=pltpu.CompilerParams(
            dimension_semantics=("parallel", "arbitrary"), vmem_limit_bytes=VMEM_LIMIT),
        name="ffn_sample" if has_hist else "ffn_prompt",
    )(*args)


def _pick_tile(S, pref):
    t = min(S, pref)
    while S % t:
        t //= 2
    return t


def kernel(x_prompt, x_sample, cache_k, cache_v, state_ret, state_conv, c_prompt, c_sample, norm1_g, norm2_g, w_ada, b_ada, w_in, w_out, attn_sinks, w_up, conv_w, conv_b, w_down, normf_g, w_ada_f, b_ada_f):
    depth = w_in.shape[0]
    B, S, D = x_prompt.shape
    Bs, Ls, _ = x_sample.shape
    assert S % WINDOW == 0 and Ls == CHUNK and cache_k.shape[2] == WINDOW

    c_all = jnp.concatenate([c_prompt, c_sample], axis=0)
    modf = _ada(c_all, w_ada_f, b_ada_f).reshape(B + Bs, 2, D)
    Tm = _pick_tile(S, 2 * ROWS_PER_STEP)
    Tf = _pick_tile(S, 2 * ROWS_PER_STEP)
    RCp = _pick_tile(Tm, 256)
    BBs = _pick_tile(Bs, ROWS_PER_STEP // Ls)

    yp, ys = x_prompt, x_sample
    outs_p, outs_s = [], []
    for l in range(depth):
        mod = _ada(c_all, w_ada[l], b_ada[l]).reshape(B + Bs, 6, D)
        win = w_in[l].astype(BF16)
        wout = w_out[l].astype(BF16)
        wup = w_up[l].astype(BF16)
        wdn = w_down[l].astype(BF16)
        final = l == depth - 1

        x1, ret_p, k_p, v_p = _mixer(yp, mod, 0, norm1_g[l], win, wout, attn_sinks[l], None,
                                     BB=1, T=Tm, RC=RCp, QG=WINDOW, NG=2)
        yp, conv_p = _ffn(x1, mod, 0, norm2_g[l], wup, conv_w[l], conv_b[l], wdn, modf, normf_g,
                          None, BB=1, T=Tf, NG=4, final=final)
        outs_p.append((ret_p, k_p.reshape(B, WINDOW, ATT_KV_HEADS, ATT_HD),
                       v_p.reshape(B, WINDOW, ATT_KV_HEADS, ATT_HD), conv_p))

        hist = (cache_k[l].reshape(Bs, WINDOW, KV_WIDTH), cache_v[l].reshape(Bs, WINDOW, KV_WIDTH),
                state_ret[l])
        x1s, ret_s, k_s, v_s = _mixer(ys, mod, B, norm1_g[l], win, wout, attn_sinks[l], hist,
                                      BB=BBs, T=Ls, RC=Ls, QG=Ls, NG=1)
        ys, conv_s = _ffn(x1s, mod, B, norm2_g[l], wup, conv_w[l], conv_b[l], wdn, modf, normf_g,
                          state_conv[l], BB=BBs, T=Ls, NG=1, final=final)
        outs_s.append((ret_s, k_s.reshape(Bs, Ls, ATT_KV_HEADS, ATT_HD),
                       v_s.reshape(Bs, Ls, ATT_KV_HEADS, ATT_HD), conv_s))

    stack = lambda lst, i: jnp.stack([t[i] for t in lst])
    return (yp, ys,
            stack(outs_p, 0), stack(outs_p, 1), stack(outs_p, 2), stack(outs_p, 3),
            stack(outs_s, 0), stack(outs_s, 1), stack(outs_s, 2), stack(outs_s, 3))
```

```python
import functools

import numpy as np
import jax
import jax.numpy as jnp
from jax import lax
from jax.experimental import pallas as pl
from jax.experimental.pallas import tpu as pltpu

F32 = jnp.float32
BF16 = jnp.bfloat16

CHUNK = 64
RET_HEADS = 4
RET_DK = 128
RET_DV = 128
RET_WIDTH = RET_HEADS * RET_DV
ATT_HEADS = 8
ATT_KV_HEADS = 2
ATT_HD = 64
ATT_REP = ATT_HEADS // ATT_KV_HEADS
ATT_WIDTH = ATT_HEADS * ATT_HD
KV_WIDTH = ATT_KV_HEADS * ATT_HD
WINDOW = 128
CONV_W = 3
EPS = 1e-6
CONV_PAD = 8
VMEM_LIMIT = 56 * 1024 * 1024
RET_LEAD = 4
ATT_LEAD = 2
ROWS_PER_STEP = 512

_OFF_RQ = 0
_OFF_RK = _OFF_RQ + RET_HEADS * RET_DK
_OFF_RV = _OFF_RK + RET_HEADS * RET_DK
_OFF_RG = _OFF_RV + RET_WIDTH
_OFF_AQ = _OFF_RG + RET_WIDTH
_OFF_AK = _OFF_AQ + ATT_WIDTH
_OFF_END = _OFF_AK + 2 * KV_WIDTH

_NT = (((1,), (1,)), ((), ()))
_TN = (((0,), (0,)), ((), ()))


def _silu(v):
    return v / (1.0 + jnp.exp(-v))


def _rms(v):
    return v * lax.rsqrt(jnp.mean(v * v, axis=-1, keepdims=True) + EPS)


def _rows_concat(parts):
    return parts[0] if len(parts) == 1 else jnp.concatenate(parts, axis=0)


def _ada_kernel(c_ref, w_ref, b_ref, o_ref):
    s = _silu(c_ref[...]).astype(BF16)
    o_ref[...] = jnp.dot(s, w_ref[...].astype(BF16), preferred_element_type=F32) + b_ref[...]


def _ada(c, w, b, tn=1024):
    m, d = c.shape
    n = w.shape[1]
    return pl.pallas_call(
        _ada_kernel,
        grid=(n // tn,),
        in_specs=[pl.BlockSpec((m, d), lambda i: (0, 0)),
                  pl.BlockSpec((d, tn), lambda i: (0, i)),
                  pl.BlockSpec((1, tn), lambda i: (0, i))],
        out_specs=pl.BlockSpec((m, tn), lambda i: (0, i)),
        out_shape=jax.ShapeDtypeStruct((m, n), F32),
        name="ada",
    )(c, w, b.reshape(1, n))


def _mixer_kernel(*refs, has_hist, BB, T, RC, QG, NG):
    if has_hist:
        (x_ref, mod_ref, ln_ref, win_ref, wout_ref, sink_ref, bias_ref, dmat_ref, qdec_ref, kdec_ref,
         gl_ref, ck_ref, cv_ref, st_ref,
         x1_ref, ret_ref, krow_ref, vrow_ref,
         state_s, kbuf, vbuf, mixbuf) = refs
    else:
        (x_ref, mod_ref, ln_ref, win_ref, wout_ref, sink_ref, bias_ref, dmat_ref, qdec_ref, kdec_ref,
         gl_ref,
         x1_ref, ret_ref, krow_ref, vrow_ref,
         state_s, kbuf, vbuf, mixbuf) = refs
    j = pl.program_id(1)
    H = WINDOW
    W = krow_ref.shape[1]

    @pl.when(j == 0)
    def _():
        if has_hist:
            state_s[...] = st_ref[...]
            kbuf[:, 0:H, :] = ck_ref[...].astype(BF16)
            vbuf[:, 0:H, :] = cv_ref[...].astype(BF16)
        else:
            state_s[...] = jnp.zeros(state_s.shape, F32)
            kbuf[:, 0:H, :] = jnp.zeros((BB, H, KV_WIDTH), BF16)
            vbuf[:, 0:H, :] = jnp.zeros((BB, H, KV_WIDTH), BF16)

    GR = (BB * T) // NG
    SEG = min(GR, T)
    proj = []
    for g in range(NG):
        hb = []
        for r in range(g * GR, (g + 1) * GR, SEG):
            bb, r0 = divmod(r, T)
            h = _rms(x_ref[bb, r0:r0 + SEG, :]) * ln_ref[...]
            hb.append((h * (1.0 + mod_ref[bb, 1:2, :]) + mod_ref[bb, 0:1, :]).astype(BF16))
        proj.append(jnp.dot(_rows_concat(hb), win_ref[...], preferred_element_type=F32))
    proj = _rows_concat(proj)
    rq = proj[:, _OFF_RQ:_OFF_RK].astype(BF16)
    rk = proj[:, _OFF_RK:_OFF_RV]
    rv = proj[:, _OFF_RV:_OFF_RG].astype(BF16)

    def ret_scores(bb, c, hh):
        rows = slice(bb * T + c * RC, bb * T + (c + 1) * RC)
        cs = slice(hh * RET_DK, (hh + 1) * RET_DK)
        q = rq[rows, cs]
        kf = rk[rows, cs]
        v = rv[rows, cs]
        s = lax.dot_general(q, kf.astype(BF16), _NT, preferred_element_type=F32)
        st = state_s[bb, hh]
        cross = jnp.dot(q, st.astype(BF16), preferred_element_type=F32)
        kd = (kf * kdec_ref[hh]).astype(BF16)
        state_s[bb, hh] = st * gl_ref[hh] + lax.dot_general(kd, v, _TN, preferred_element_type=F32)
        return rows, cs, hh, s, cross, v

    def ret_finish(ctx):
        rows, cs, hh, s, cross, v = ctx
        o = (jnp.dot((s * dmat_ref[hh]).astype(BF16), v, preferred_element_type=F32)
             + cross * qdec_ref[hh])
        mixbuf[rows, cs] = (_silu(rg[rows, cs]) * _rms(o)).astype(BF16)

    def att_scores(bb, p):
        q0 = bb * T + p * QG
        ss = []
        for g in range(ATT_KV_HEADS):
            q4 = jnp.concatenate(
                [aq[q0:q0 + QG, (g * ATT_REP + r) * ATT_HD:(g * ATT_REP + r + 1) * ATT_HD]
                 for r in range(ATT_REP)], axis=0)
            kb = kbuf[bb, p * QG:p * QG + QG + H, g * ATT_HD:(g + 1) * ATT_HD]
            ss.append(lax.dot_general(kb, q4, _NT, preferred_element_type=F32))
        return bb, p, ss

    def att_finish(ctx):
        bb, p, ss = ctx
        q0 = bb * T + p * QG
        pn = []
        for g, s in enumerate(ss):
            if (not has_hist) and p == 0:
                bias = bias_ref[g + ATT_KV_HEADS * (j == 0).astype(jnp.int32)]
            else:
                bias = bias_ref[g]
            s = s + bias
            sk = sink_ref[g]
            m = jnp.maximum(jnp.max(s, axis=0, keepdims=True), sk)
            pe = jnp.exp(s - m)
            den = jnp.sum(pe, axis=0, keepdims=True) + jnp.exp(sk - m)
            pn.append((pe * (1.0 / den)).astype(BF16))
        o = lax.dot_general(jnp.concatenate(pn, axis=1), vbuf[bb, p * QG:p * QG + QG + H, :], _TN,
                            preferred_element_type=F32)
        for g in range(ATT_KV_HEADS):
            for r in range(ATT_REP):
                c0 = RET_WIDTH + (g * ATT_REP + r) * ATT_HD
                r0 = (g * ATT_REP + r) * QG
                mixbuf[q0:q0 + QG, c0:c0 + ATT_HD] = o[r0:r0 + QG, g * ATT_HD:(g + 1) * ATT_HD].astype(BF16)

    ret_tasks = [(bb, c, hh) for c in range(T // RC) for bb in range(BB) for hh in range(RET_HEADS)]
    att_tasks = [(bb, p) for bb in range(BB) for p in range(T // QG)]
    ret_ctx = [ret_scores(*t) for t in ret_tasks[:RET_LEAD]]
    rg = proj[:, _OFF_RG:_OFF_AQ]
    aq = (proj[:, _OFF_AQ:_OFF_AK] * (ATT_HD ** -0.5)).astype(BF16)
    ak = proj[:, _OFF_AK:_OFF_AK + KV_WIDTH]
    av = proj[:, _OFF_AK + KV_WIDTH:_OFF_END]
    for bb in range(BB):
        rows = slice(bb * T, (bb + 1) * T)
        kbuf[bb, H:H + T, :] = ak[rows, :].astype(BF16)
        vbuf[bb, H:H + T, :] = av[rows, :].astype(BF16)
        krow_ref[bb] = ak[(bb + 1) * T - W:(bb + 1) * T, :]
        vrow_ref[bb] = av[(bb + 1) * T - W:(bb + 1) * T, :]
    att_ctx = []
    for i in range(len(ret_tasks)):
        if len(ret_ctx) < len(ret_tasks):
            ret_ctx.append(ret_scores(*ret_tasks[len(ret_ctx)]))
        elif len(att_ctx) < min(ATT_LEAD, len(att_tasks)):
            att_ctx.append(att_scores(*att_tasks[len(att_ctx)]))
        ret_finish(ret_ctx[i])
        ret_ctx[i] = None
    def out_group(g):
        mix = jnp.dot(mixbuf[g * GR:(g + 1) * GR, :], wout_ref[...], preferred_element_type=F32)
        for r in range(g * GR, (g + 1) * GR, SEG):
            bb, r0 = divmod(r, T)
            x1_ref[bb, r0:r0 + SEG, :] = (x_ref[bb, r0:r0 + SEG, :]
                                          + mod_ref[bb, 2:3, :] * mix[r - g * GR:r - g * GR + SEG, :])

    done = 0
    for i in range(len(att_tasks)):
        if len(att_ctx) < len(att_tasks):
            att_ctx.append(att_scores(*att_tasks[len(att_ctx)]))
        att_finish(att_ctx[i])
        att_ctx[i] = None
        bb, p = att_tasks[i]
        rows_done = bb * T + (p + 1) * QG
        while (done + 1) * GR <= rows_done:
            out_group(done)
            done += 1
    assert done == NG

    ret_ref[...] = state_s[...]
    if T >= H:
        kbuf[:, 0:H, :] = kbuf[:, T:T + H, :]
        vbuf[:, 0:H, :] = vbuf[:, T:T + H, :]


def _retention_consts(RC):
    lg = np.log1p(-np.exp2(-5.0 - np.arange(RET_HEADS, dtype=np.float64)))
    i = np.arange(RC, dtype=np.float64)
    diff = i[:, None] - i[None, :]
    scale = RET_DK ** -0.5
    dmat = np.where(diff >= 0, np.exp(lg[:, None, None] * np.maximum(diff, 0.0)), 0.0) * scale
    qdec = np.exp(lg[:, None] * (i[None, :] + 1.0))
    kdec = np.exp(lg[:, None] * (RC - 1.0 - i[None, :])) * scale
    gl = np.exp(lg * RC)
    bc = lambda t: np.broadcast_to(t[:, :, None], (RET_HEADS, t.shape[1], RET_DV))
    f32 = lambda t: jnp.asarray(np.ascontiguousarray(t), F32)
    return (f32(dmat), f32(bc(qdec)), f32(bc(kdec)),
            f32(np.broadcast_to(gl[:, None, None], (RET_HEADS, 1, RET_DV))))


def _attention_bias(QG, with_first):
    KB = QG + WINDOW
    qi = np.arange(QG)[:, None]
    ki = np.arange(KB)[None, :] - WINDOW
    qc = qi // CHUNK
    kc = np.floor_divide(ki, CHUNK)
    vis = (kc <= qc) & (kc >= qc - WINDOW // CHUNK)
    dist = np.abs(qi - ki).astype(np.float64)
    slopes = np.exp2(-8.0 * np.arange(1, ATT_HEADS + 1, dtype=np.float64) / ATT_HEADS)
    b = -slopes[:, None, None] * dist[None]
    b = np.where(vis[None], b, -np.inf)
    lay = lambda t: t.reshape(ATT_KV_HEADS, ATT_REP * QG, KB).swapaxes(1, 2)
    out = lay(b)
    if with_first:
        out = np.concatenate([out, lay(np.where((ki >= 0)[None], b, -np.inf))], axis=0)
    return jnp.asarray(np.ascontiguousarray(out), F32)


def _const_spec(shape):
    nd = len(shape)
    return pl.BlockSpec(shape, lambda b, j: (0,) * nd, pipeline_mode=pl.Buffered(1))


def _seq_spec(BB, *tail, first=0):
    assert first % BB == 0
    return pl.BlockSpec((BB,) + tail, lambda b, j: (b + first // BB,) + (0,) * len(tail))


def _mixer(x, mod, mod_first, ln, w_in, w_out, sinks, hist, BB, T, RC, QG, NG):
    B, S, D = x.shape
    has_hist = hist is not None
    H = WINDOW
    dmat, qdec, kdec, gl = _retention_consts(RC)
    bias = _attention_bias(QG, not has_hist)
    sinkrow = jnp.repeat(sinks.astype(F32).reshape(ATT_KV_HEADS, ATT_REP, 1), QG, axis=2)
    sinkrow = sinkrow.reshape(ATT_KV_HEADS, 1, ATT_REP * QG)
    W = T if has_hist else H

    x_spec = pl.BlockSpec((BB, T, D), lambda b, j: (b, j, 0))
    in_specs = [
        x_spec,
        _seq_spec(BB, 6, D, first=mod_first),
        _const_spec((1, D)),
        _const_spec(w_in.shape),
        _const_spec(w_out.shape),
        _const_spec(sinkrow.shape),
        _const_spec(bias.shape),
        _const_spec(dmat.shape),
        _const_spec(qdec.shape),
        _const_spec(kdec.shape),
        _const_spec(gl.shape),
    ]
    args = [x, mod, ln.reshape(1, D), w_in, w_out, sinkrow, bias, dmat, qdec, kdec, gl]
    if has_hist:
        in_specs += [_seq_spec(BB, H, KV_WIDTH), _seq_spec(BB, H, KV_WIDTH),
                     _seq_spec(BB, RET_HEADS, RET_DK, RET_DV)]
        args += list(hist)
    return pl.pallas_call(
        functools.partial(_mixer_kernel, has_hist=has_hist, BB=BB, T=T, RC=RC, QG=QG, NG=NG),
        grid=(B // BB, S // T),
        in_specs=in_specs,
        out_specs=[x_spec, _seq_spec(BB, RET_HEADS, RET_DK, RET_DV),
                   _seq_spec(BB, W, KV_WIDTH), _seq_spec(BB, W, KV_WIDTH)],
        out_shape=[jax.ShapeDtypeStruct((B, S, D), F32),
                   jax.ShapeDtypeStruct((B, RET_HEADS, RET_DK, RET_DV), F32),
                   jax.ShapeDtypeStruct((B, W, KV_WIDTH), F32),
                   jax.ShapeDtypeStruct((B, W, KV_WIDTH), F32)],
        scratch_shapes=[pltpu.VMEM((BB, RET_HEADS, RET_DK, RET_DV), F32),
                        pltpu.VMEM((BB, H + T, KV_WIDTH), BF16),
                        pltpu.VMEM((BB, H + T, KV_WIDTH), BF16),
                        pltpu.VMEM((BB * T, D), BF16)],
        compiler_params=pltpu.CompilerParams(
            dimension_semantics=("parallel", "arbitrary"), vmem_limit_bytes=VMEM_LIMIT),
        name="mixer_sample" if has_hist else "mixer_prompt",
    )(*args)


def _ffn_kernel(*refs, has_hist, BB, T, NG, final):
    if has_hist:
        (x_ref, mod_ref, ln_ref, wup_ref, cw_ref, cb_ref, wdn_ref, modf_ref, lnf_ref, cs_ref,
         y_ref, cso_ref, prev_s) = refs
    else:
        (x_ref, mod_ref, ln_ref, wup_ref, cw_ref, cb_ref, wdn_ref, modf_ref, lnf_ref,
         y_ref, cso_ref, prev_s) = refs
    j = pl.program_id(1)
    F = cw_ref.shape[-1]

    @pl.when(j == 0)
    def _():
        prev_s[...] = jnp.zeros(prev_s.shape, F32)
        if has_hist:
            prev_s[:, CONV_PAD - (CONV_W - 1):, :] = cs_ref[...]

    def shifted(a, k, tail):
        r = pltpu.roll(a, k, 0)
        row = lax.broadcasted_iota(jnp.int32, tail.shape, 0)
        head = jnp.where(row < k, pltpu.roll(tail, k, 0), r[0:CONV_PAD, :])
        return jnp.concatenate([head, r[CONV_PAD:, :]], axis=0)

    GR = (BB * T) // NG
    if BB == 1:
        groups = [[(0, g * GR, (g + 1) * GR)] for g in range(NG)]
    else:
        groups = [[(bb, 0, T) for bb in range(g * (GR // T), (g + 1) * (GR // T))] for g in range(NG)]
    up = []
    for segs in groups:
        hb = []
        for bb, r0, r1 in segs:
            h = _rms(x_ref[bb, r0:r1, :]) * ln_ref[...]
            hb.append((h * (1.0 + mod_ref[bb, 4:5, :]) + mod_ref[bb, 3:4, :]).astype(BF16))
        hb = _rows_concat(hb)
        up.append((jnp.dot(hb, wup_ref[:, :F], preferred_element_type=F32),
                   jnp.dot(hb, wup_ref[:, F:], preferred_element_type=F32)))
    def finish(segs, f):
        o = 0
        for bb, r0, r1 in segs:
            x2 = x_ref[bb, r0:r1, :] + mod_ref[bb, 5:6, :] * f[o:o + r1 - r0, :]
            if final:
                x2 = (_rms(x2) * lnf_ref[...]) * (1.0 + modf_ref[bb, 1:2, :]) + modf_ref[bb, 0:1, :]
            y_ref[bb, r0:r1, :] = x2
            o += r1 - r0

    tails = [prev_s[bb] for bb in range(BB)]
    ffn = []
    for segs, (a, b) in zip(groups, up):
        conv = []
        o = 0
        for bb, r0, r1 in segs:
            a_s = a[o:o + r1 - r0, :]
            conv.append(cb_ref[...] + (shifted(a_s, 2, tails[bb]) * cw_ref[0:1, :]
                                       + shifted(a_s, 1, tails[bb]) * cw_ref[1:2, :] + a_s * cw_ref[2:3, :]))
            tails[bb] = a_s[r1 - r0 - CONV_PAD:r1 - r0, :]
            o += r1 - r0
        gate = (_silu(_rows_concat(conv)) * b).astype(BF16)
        ffn.append(jnp.dot(gate, wdn_ref[...], preferred_element_type=F32))
    for bb in range(BB):
        prev_s[bb] = tails[bb]
    for segs, f in zip(groups, ffn):
        finish(segs, f)

    cso_ref[...] = prev_s[:, CONV_PAD - (CONV_W - 1):, :]


def _ffn(x1, mod, mod_first, ln, w_up, conv_w, conv_b, w_down, modf, lnf, conv_state, BB, T, NG, final):
    B, S, D = x1.shape
    F = conv_w.shape[-1]
    has_hist = conv_state is not None
    x_spec = pl.BlockSpec((BB, T, D), lambda b, j: (b, j, 0))
    in_specs = [
        x_spec,
        _seq_spec(BB, 6, D, first=mod_first),
        _const_spec((1, D)),
        _const_spec(w_up.shape),
        _const_spec(conv_w.shape),
        _const_spec((1, F)),
        _const_spec(w_down.shape),
        _seq_spec(BB, 2, D, first=mod_first),
        _const_spec((1, D)),
    ]
    args = [x1, mod, ln.reshape(1, D), w_up, conv_w, conv_b.reshape(1, F), w_down, modf, lnf.reshape(1, D)]
    if has_hist:
        in_specs.append(_seq_spec(BB, CONV_W - 1, F))
        args.append(conv_state)
    return pl.pallas_call(
        functools.partial(_ffn_kernel, has_hist=has_hist, BB=BB, T=T, NG=NG, final=final),
        grid=(B // BB, S // T),
        in_specs=in_specs,
        out_specs=[x_spec, _seq_spec(BB, CONV_W - 1, F)],
        out_shape=[jax.ShapeDtypeStruct((B, S, D), F32),
                   jax.ShapeDtypeStruct((B, CONV_W - 1, F), F32)],
        scratch_shapes=[pltpu.VMEM((BB, CONV_PAD, F), F32)],
        compiler_params=pltpu.CompilerParams(
            dimension_semantics=("parallel", "arbitrary"), vmem_limit_bytes=VMEM_LIMIT),
        name="ffn_sample" if has_hist else "ffn_prompt",
    )(*args)


def _pick_tile(S, pref):
    t = min(S, pref)
    while S % t:
        t //= 2
    return t


def kernel(x_prompt, x_sample, cache_k, cache_v, state_ret, state_conv, c_prompt, c_sample, norm1_g, norm2_g, w_ada, b_ada, w_in, w_out, attn_sinks, w_up, conv_w, conv_b, w_down, normf_g, w_ada_f, b_ada_f):
    depth = w_in.shape[0]
    B, S, D = x_prompt.shape
    Bs, Ls, _ = x_sample.shape
    assert S % WINDOW == 0 and Ls == CHUNK and cache_k.shape[2] == WINDOW

    c_all = jnp.concatenate([c_prompt, c_sample], axis=0)
    modf = _ada(c_all, w_ada_f, b_ada_f).reshape(B + Bs, 2, D)
    Tm = _pick_tile(S, 2 * ROWS_PER_STEP)
    Tf = _pick_tile(S, 2 * ROWS_PER_STEP)
    RCp = _pick_tile(Tm, 256)
    BBs = _pick_tile(Bs, ROWS_PER_STEP // Ls)

    yp, ys = x_prompt, x_sample
    outs_p, outs_s = [], []
    for l in range(depth):
        mod = _ada(c_all, w_ada[l], b_ada[l]).reshape(B + Bs, 6, D)
        win = w_in[l].astype(BF16)
        wout = w_out[l].astype(BF16)
        wup = w_up[l].astype(BF16)
        wdn = w_down[l].astype(BF16)
        final = l == depth - 1

        x1, ret_p, k_p, v_p = _mixer(yp, mod, 0, norm1_g[l], win, wout, attn_sinks[l], None,
                                     BB=1, T=Tm, RC=RCp, QG=WINDOW, NG=2)
        yp, conv_p = _ffn(x1, mod, 0, norm2_g[l], wup, conv_w[l], conv_b[l], wdn, modf, normf_g,
                          None, BB=1, T=Tf, NG=4, final=final)
        outs_p.append((ret_p, k_p.reshape(B, WINDOW, ATT_KV_HEADS, ATT_HD),
                       v_p.reshape(B, WINDOW, ATT_KV_HEADS, ATT_HD), conv_p))

        hist = (cache_k[l].reshape(Bs, WINDOW, KV_WIDTH), cache_v[l].reshape(Bs, WINDOW, KV_WIDTH),
                state_ret[l])
        x1s, ret_s, k_s, v_s = _mixer(ys, mod, B, norm1_g[l], win, wout, attn_sinks[l], hist,
                                      BB=BBs, T=Ls, RC=Ls, QG=Ls, NG=1)
        ys, conv_s = _ffn(x1s, mod, B, norm2_g[l], wup, conv_w[l], conv_b[l], wdn, modf, normf_g,
                          state_conv[l], BB=BBs, T=Ls, NG=1, final=final)
        outs_s.append((ret_s, k_s.reshape(Bs, Ls, ATT_KV_HEADS, ATT_HD),
                       v_s.reshape(Bs, Ls, ATT_KV_HEADS, ATT_HD), conv_s))

    stack = lambda lst, i: jnp.stack([t[i] for t in lst])
    return (yp, ys,
            stack(outs_p, 0), stack(outs_p, 1), stack(outs_p, 2), stack(outs_p, 3),
            stack(outs_s, 0), stack(outs_s, 1), stack(outs_s, 2), stack(outs_s, 3))
```

```python
import functools

import numpy as np
import jax
import jax.numpy as jnp
from jax import lax
from jax.experimental import pallas as pl
from jax.experimental.pallas import tpu as pltpu

F32 = jnp.float32
BF16 = jnp.bfloat16

CHUNK = 64
RET_HEADS = 4
RET_DK = 128
RET_DV = 128
RET_WIDTH = RET_HEADS * RET_DV
ATT_HEADS = 8
ATT_KV_HEADS = 2
ATT_HD = 64
ATT_REP = ATT_HEADS // ATT_KV_HEADS
ATT_WIDTH = ATT_HEADS * ATT_HD
KV_WIDTH = ATT_KV_HEADS * ATT_HD
WINDOW = 128
CONV_W = 3
EPS = 1e-6
CONV_PAD = 8
VMEM_LIMIT = 56 * 1024 * 1024
RET_LEAD = 4
ATT_LEAD = 2
ROWS_PER_STEP = 512

_OFF_RQ = 0
_OFF_RK = _OFF_RQ + RET_HEADS * RET_DK
_OFF_RV = _OFF_RK + RET_HEADS * RET_DK
_OFF_RG = _OFF_RV + RET_WIDTH
_OFF_AQ = _OFF_RG + RET_WIDTH
_OFF_AK = _OFF_AQ + ATT_WIDTH
_OFF_END = _OFF_AK + 2 * KV_WIDTH

_NT = (((1,), (1,)), ((), ()))
_TN = (((0,), (0,)), ((), ()))


def _silu(v):
    return v / (1.0 + jnp.exp(-v))


def _rms(v):
    return v * lax.rsqrt(jnp.mean(v * v, axis=-1, keepdims=True) + EPS)


def _rows_concat(parts):
    return parts[0] if len(parts) == 1 else jnp.concatenate(parts, axis=0)


def _ada_kernel(c_ref, w_ref, b_ref, o_ref):
    s = _silu(c_ref[...]).astype(BF16)
    o_ref[...] = jnp.dot(s, w_ref[...].astype(BF16), preferred_element_type=F32) + b_ref[...]


def _ada(c, w, b, tn=1024):
    m, d = c.shape
    n = w.shape[1]
    return pl.pallas_call(
        _ada_kernel,
        grid=(n // tn,),
        in_specs=[pl.BlockSpec((m, d), lambda i: (0, 0)),
                  pl.BlockSpec((d, tn), lambda i: (0, i)),
                  pl.BlockSpec((1, tn), lambda i: (0, i))],
        out_specs=pl.BlockSpec((m, tn), lambda i: (0, i)),
        out_shape=jax.ShapeDtypeStruct((m, n), F32),
        name="ada",
    )(c, w, b.reshape(1, n))


def _mixer_kernel(*refs, has_hist, BB, T, RC, QG, NG):
    if has_hist:
        (x_ref, mod_ref, ln_ref, win_ref, wout_ref, sink_ref, bias_ref, dmat_ref, qdec_ref, kdec_ref,
         gl_ref, ck_ref, cv_ref, st_ref,
         x1_ref, ret_ref, krow_ref, vrow_ref,
         state_s, kbuf, vbuf, mixbuf) = refs
    else:
        (x_ref, mod_ref, ln_ref, win_ref, wout_ref, sink_ref, bias_ref, dmat_ref, qdec_ref, kdec_ref,
         gl_ref,
         x1_ref, ret_ref, krow_ref, vrow_ref,
         state_s, kbuf, vbuf, mixbuf) = refs
    j = pl.program_id(1)
    H = WINDOW
    W = krow_ref.shape[1]

    @pl.when(j == 0)
    def _():
        if has_hist:
            state_s[...] = st_ref[...]
            kbuf[:, 0:H, :] = ck_ref[...].astype(BF16)
            vbuf[:, 0:H, :] = cv_ref[...].astype(BF16)
        else:
            state_s[...] = jnp.zeros(state_s.shape, F32)
            kbuf[:, 0:H, :] = jnp.zeros((BB, H, KV_WIDTH), BF16)
            vbuf[:, 0:H, :] = jnp.zeros((BB, H, KV_WIDTH), BF16)

    GR = (BB * T) // NG
    SEG = min(GR, T)
    proj = []
    for g in range(NG):
        hb = []
        for r in range(g * GR, (g + 1) * GR, SEG):
            bb, r0 = divmod(r, T)
            h = _rms(x_ref[bb, r0:r0 + SEG, :]) * ln_ref[...]
            hb.append((h * (1.0 + mod_ref[bb, 1:2, :]) + mod_ref[bb, 0:1, :]).astype(BF16))
        proj.append(jnp.dot(_rows_concat(hb), win_ref[...], preferred_element_type=F32))
    proj = _rows_concat(proj)
    rq = proj[:, _OFF_RQ:_OFF_RK].astype(BF16)
    rk = proj[:, _OFF_RK:_OFF_RV]
    rv = proj[:, _OFF_RV:_OFF_RG].astype(BF16)

    def ret_scores(bb, c, hp):
        rows = slice(bb * T + c * RC, bb * T + (c + 1) * RC)
        heads = (2 * hp, 2 * hp + 1)
        st = [state_s[bb, hh] for hh in heads]
        zero = jnp.zeros((RET_DK, RET_DV), BF16)
        st2 = jnp.concatenate([jnp.concatenate([st[0].astype(BF16), zero], axis=1),
                               jnp.concatenate([zero, st[1].astype(BF16)], axis=1)], axis=0)
        cross2 = jnp.dot(rq[rows, heads[0] * RET_DK:(heads[1] + 1) * RET_DK], st2, preferred_element_type=F32)
        out = []
        for i, hh in enumerate(heads):
            cs = slice(hh * RET_DK, (hh + 1) * RET_DK)
            q = rq[rows, cs]
            kf = rk[rows, cs]
            v = rv[rows, cs]
            s = lax.dot_general(q, kf.astype(BF16), _NT, preferred_element_type=F32)
            kd = (kf * kdec_ref[hh]).astype(BF16)
            state_s[bb, hh] = st[i] * gl_ref[hh] + lax.dot_general(kd, v, _TN, preferred_element_type=F32)
            out.append((rows, cs, hh, s, cross2[:, i * RET_DV:(i + 1) * RET_DV], v))
        return out

    def ret_finish(ctxs):
        for rows, cs, hh, s, cross, v in ctxs:
            o = (jnp.dot((s * dmat_ref[hh]).astype(BF16), v, preferred_element_type=F32)
                 + cross * qdec_ref[hh])
            mixbuf[rows, cs] = (_silu(rg[rows, cs]) * _rms(o)).astype(BF16)

    def att_scores(bb, p):
        q0 = bb * T + p * QG
        ss = []
        for g in range(ATT_KV_HEADS):
            q4 = jnp.concatenate(
                [aq[q0:q0 + QG, (g * ATT_REP + r) * ATT_HD:(g * ATT_REP + r + 1) * ATT_HD]
                 for r in range(ATT_REP)], axis=0)
            kb = kbuf[bb, p * QG:p * QG + QG + H, g * ATT_HD:(g + 1) * ATT_HD]
            ss.append(lax.dot_general(kb, q4, _NT, preferred_element_type=F32))
        return bb, p, ss

    def att_finish(ctx):
        bb, p, ss = ctx
        q0 = bb * T + p * QG
        pn = []
        for g, s in enumerate(ss):
            if (not has_hist) and p == 0:
                bias = bias_ref[g + ATT_KV_HEADS * (j == 0).astype(jnp.int32)]
            else:
                bias = bias_ref[g]
            s = s + bias
            sk = sink_ref[g]
            m = jnp.maximum(jnp.max(s, axis=0, keepdims=True), sk)
            pe = jnp.exp(s - m)
            den = jnp.sum(pe, axis=0, keepdims=True) + jnp.exp(sk - m)
            pn.append((pe * (1.0 / den)).astype(BF16))
        o = lax.dot_general(jnp.concatenate(pn, axis=1), vbuf[bb, p * QG:p * QG + QG + H, :], _TN,
                            preferred_element_type=F32)
        for g in range(ATT_KV_HEADS):
            for r in range(ATT_REP):
                c0 = RET_WIDTH + (g * ATT_REP + r) * ATT_HD
                r0 = (g * ATT_REP + r) * QG
                mixbuf[q0:q0 + QG, c0:c0 + ATT_HD] = o[r0:r0 + QG, g * ATT_HD:(g + 1) * ATT_HD].astype(BF16)

    ret_tasks = [(bb, c, hp) for c in range(T // RC) for bb in range(BB) for hp in range(RET_HEADS // 2)]
    att_tasks = [(bb, p) for bb in range(BB) for p in range(T // QG)]
    ret_ctx = [ret_scores(*t) for t in ret_tasks[:RET_LEAD // 2]]
    rg = proj[:, _OFF_RG:_OFF_AQ]
    aq = (proj[:, _OFF_AQ:_OFF_AK] * (ATT_HD ** -0.5)).astype(BF16)
    ak = proj[:, _OFF_AK:_OFF_AK + KV_WIDTH]
    av = proj[:, _OFF_AK + KV_WIDTH:_OFF_END]
    for bb in range(BB):
        rows = slice(bb * T, (bb + 1) * T)
        kbuf[bb, H:H + T, :] = ak[rows, :].astype(BF16)
        vbuf[bb, H:H + T, :] = av[rows, :].astype(BF16)
        krow_ref[bb] = ak[(bb + 1) * T - W:(bb + 1) * T, :]
        vrow_ref[bb] = av[(bb + 1) * T - W:(bb + 1) * T, :]
    att_ctx = []
    for i in range(len(ret_tasks)):
        if len(ret_ctx) < len(ret_tasks):
            ret_ctx.append(ret_scores(*ret_tasks[len(ret_ctx)]))
        elif len(att_ctx) < min(ATT_LEAD, len(att_tasks)):
            att_ctx.append(att_scores(*att_tasks[len(att_ctx)]))
        ret_finish(ret_ctx[i])
        ret_ctx[i] = None
    def out_group(g):
        mix = jnp.dot(mixbuf[g * GR:(g + 1) * GR, :], wout_ref[...], preferred_element_type=F32)
        for r in range(g * GR, (g + 1) * GR, SEG):
            bb, r0 = divmod(r, T)
            x1_ref[bb, r0:r0 + SEG, :] = (x_ref[bb, r0:r0 + SEG, :]
                                          + mod_ref[bb, 2:3, :] * mix[r - g * GR:r - g * GR + SEG, :])

    done = 0
    for i in range(len(att_tasks)):
        if len(att_ctx) < len(att_tasks):
            att_ctx.append(att_scores(*att_tasks[len(att_ctx)]))
        att_finish(att_ctx[i])
        att_ctx[i] = None
        bb, p = att_tasks[i]
        rows_done = bb * T + (p + 1) * QG
        while (done + 1) * GR <= rows_done:
            out_group(done)
            done += 1
    assert done == NG

    ret_ref[...] = state_s[...]
    if T >= H:
        kbuf[:, 0:H, :] = kbuf[:, T:T + H, :]
        vbuf[:, 0:H, :] = vbuf[:, T:T + H, :]


def _retention_consts(RC):
    lg = np.log1p(-np.exp2(-5.0 - np.arange(RET_HEADS, dtype=np.float64)))
    i = np.arange(RC, dtype=np.float64)
    diff = i[:, None] - i[None, :]
    scale = RET_DK ** -0.5
    dmat = np.where(diff >= 0, np.exp(lg[:, None, None] * np.maximum(diff, 0.0)), 0.0) * scale
    qdec = np.exp(lg[:, None] * (i[None, :] + 1.0))
    kdec = np.exp(lg[:, None] * (RC - 1.0 - i[None, :])) * scale
    gl = np.exp(lg * RC)
    bc = lambda t: np.broadcast_to(t[:, :, None], (RET_HEADS, t.shape[1], RET_DV))
    f32 = lambda t: jnp.asarray(np.ascontiguousarray(t), F32)
    return (f32(dmat), f32(bc(qdec)), f32(bc(kdec)),
            f32(np.broadcast_to(gl[:, None, None], (RET_HEADS, 1, RET_DV))))


def _attention_bias(QG, with_first):
    KB = QG + WINDOW
    qi = np.arange(QG)[:, None]
    ki = np.arange(KB)[None, :] - WINDOW
    qc = qi // CHUNK
    kc = np.floor_divide(ki, CHUNK)
    vis = (kc <= qc) & (kc >= qc - WINDOW // CHUNK)
    dist = np.abs(qi - ki).astype(np.float64)
    slopes = np.exp2(-8.0 * np.arange(1, ATT_HEADS + 1, dtype=np.float64) / ATT_HEADS)
    b = -slopes[:, None, None] * dist[None]
    b = np.where(vis[None], b, -np.inf)
    lay = lambda t: t.reshape(ATT_KV_HEADS, ATT_REP * QG, KB).swapaxes(1, 2)
    out = lay(b)
    if with_first:
        out = np.concatenate([out, lay(np.where((ki >= 0)[None], b, -np.inf))], axis=0)
    return jnp.asarray(np.ascontiguousarray(out), F32)


def _const_spec(shape):
    nd = len(shape)
    return pl.BlockSpec(shape, lambda b, j: (0,) * nd, pipeline_mode=pl.Buffered(1))


def _seq_spec(BB, *tail, first=0):
    assert first % BB == 0
    return pl.BlockSpec((BB,) + tail, lambda b, j: (b + first // BB,) + (0,) * len(tail))


def _mixer(x, mod, mod_first, ln, w_in, w_out, sinks, hist, BB, T, RC, QG, NG):
    B, S, D = x.shape
    has_hist = hist is not None
    H = WINDOW
    dmat, qdec, kdec, gl = _retention_consts(RC)
    bias = _attention_bias(QG, not has_hist)
    sinkrow = jnp.repeat(sinks.astype(F32).reshape(ATT_KV_HEADS, ATT_REP, 1), QG, axis=2)
    sinkrow = sinkrow.reshape(ATT_KV_HEADS, 1, ATT_REP * QG)
    W = T if has_hist else H

    x_spec = pl.BlockSpec((BB, T, D), lambda b, j: (b, j, 0))
    in_specs = [
        x_spec,
        _seq_spec(BB, 6, D, first=mod_first),
        _const_spec((1, D)),
        _const_spec(w_in.shape),
        _const_spec(w_out.shape),
        _const_spec(sinkrow.shape),
        _const_spec(bias.shape),
        _const_spec(dmat.shape),
        _const_spec(qdec.shape),
        _const_spec(kdec.shape),
        _const_spec(gl.shape),
    ]
    args = [x, mod, ln.reshape(1, D), w_in, w_out, sinkrow, bias, dmat, qdec, kdec, gl]
    if has_hist:
        in_specs += [_seq_spec(BB, H, KV_WIDTH), _seq_spec(BB, H, KV_WIDTH),
                     _seq_spec(BB, RET_HEADS, RET_DK, RET_DV)]
        args += list(hist)
    return pl.pallas_call(
        functools.partial(_mixer_kernel, has_hist=has_hist, BB=BB, T=T, RC=RC, QG=QG, NG=NG),
        grid=(B // BB, S // T),
        in_specs=in_specs,
        out_specs=[x_spec, _seq_spec(BB, RET_HEADS, RET_DK, RET_DV),
                   _seq_spec(BB, W, KV_WIDTH), _seq_spec(BB, W, KV_WIDTH)],
        out_shape=[jax.ShapeDtypeStruct((B, S, D), F32),
                   jax.ShapeDtypeStruct((B, RET_HEADS, RET_DK, RET_DV), F32),
                   jax.ShapeDtypeStruct((B, W, KV_WIDTH), F32),
                   jax.ShapeDtypeStruct((B, W, KV_WIDTH), F32)],
        scratch_shapes=[pltpu.VMEM((BB, RET_HEADS, RET_DK, RET_DV), F32),
                        pltpu.VMEM((BB, H + T, KV_WIDTH), BF16),
                        pltpu.VMEM((BB, H + T, KV_WIDTH), BF16),
                        pltpu.VMEM((BB * T, D), BF16)],
        compiler_params=pltpu.CompilerParams(
            dimension_semantics=("parallel", "arbitrary"), vmem_limit_bytes=VMEM_LIMIT),
        name="mixer_sample" if has_hist else "mixer_prompt",
    )(*args)


def _ffn_kernel(*refs, has_hist, BB, T, NG, final):
    if has_hist:
        (x_ref, mod_ref, ln_ref, wup_ref, cw_ref, cb_ref, wdn_ref, modf_ref, lnf_ref, cs_ref,
         y_ref, cso_ref, prev_s) = refs
    else:
        (x_ref, mod_ref, ln_ref, wup_ref, cw_ref, cb_ref, wdn_ref, modf_ref, lnf_ref,
         y_ref, cso_ref, prev_s) = refs
    j = pl.program_id(1)
    F = cw_ref.shape[-1]

    @pl.when(j == 0)
    def _():
        prev_s[...] = jnp.zeros(prev_s.shape, F32)
        if has_hist:
            prev_s[:, CONV_PAD - (CONV_W - 1):, :] = cs_ref[...]

    def shifted(a, k, tail):
        r = pltpu.roll(a, k, 0)
        row = lax.broadcasted_iota(jnp.int32, tail.shape, 0)
        head = jnp.where(row < k, pltpu.roll(tail, k, 0), r[0:CONV_PAD, :])
        return jnp.concatenate([head, r[CONV_PAD:, :]], axis=0)

    GR = (BB * T) // NG
    if BB == 1:
        groups = [[(0, g * GR, (g + 1) * GR)] for g in range(NG)]
    else:
        groups = [[(bb, 0, T) for bb in range(g * (GR // T), (g + 1) * (GR // T))] for g in range(NG)]
    up = []
    for segs in groups:
        hb = []
        for bb, r0, r1 in segs:
            h = _rms(x_ref[bb, r0:r1, :]) * ln_ref[...]
            hb.append((h * (1.0 + mod_ref[bb, 4:5, :]) + mod_ref[bb, 3:4, :]).astype(BF16))
        hb = _rows_concat(hb)
        up.append((jnp.dot(hb, wup_ref[:, :F], preferred_element_type=F32),
                   jnp.dot(hb, wup_ref[:, F:], preferred_element_type=F32)))
    def finish(segs, f):
        o = 0
        for bb, r0, r1 in segs:
            x2 = x_ref[bb, r0:r1, :] + mod_ref[bb, 5:6, :] * f[o:o + r1 - r0, :]
            if final:
                x2 = (_rms(x2) * lnf_ref[...]) * (1.0 + modf_ref[bb, 1:2, :]) + modf_ref[bb, 0:1, :]
            y_ref[bb, r0:r1, :] = x2
            o += r1 - r0

    tails = [prev_s[bb] for bb in range(BB)]
    ffn = []
    for segs, (a, b) in zip(groups, up):
        conv = []
        o = 0
        for bb, r0, r1 in segs:
            a_s = a[o:o + r1 - r0, :]
            conv.append(cb_ref[...] + (shifted(a_s, 2, tails[bb]) * cw_ref[0:1, :]
                                       + shifted(a_s, 1, tails[bb]) * cw_ref[1:2, :] + a_s * cw_ref[2:3, :]))
            tails[bb] = a_s[r1 - r0 - CONV_PAD:r1 - r0, :]
            o += r1 - r0
        gate = (_silu(_rows_concat(conv)) * b).astype(BF16)
        ffn.append(jnp.dot(gate, wdn_ref[...], preferred_element_type=F32))
    for bb in range(BB):
        prev_s[bb] = tails[bb]
    for segs, f in zip(groups, ffn):
        finish(segs, f)

    cso_ref[...] = prev_s[:, CONV_PAD - (CONV_W - 1):, :]


def _ffn(x1, mod, mod_first, ln, w_up, conv_w, conv_b, w_down, modf, lnf, conv_state, BB, T, NG, final):
    B, S, D = x1.shape
    F = conv_w.shape[-1]
    has_hist = conv_state is not None
    x_spec = pl.BlockSpec((BB, T, D), lambda b, j: (b, j, 0))
    in_specs = [
        x_spec,
        _seq_spec(BB, 6, D, first=mod_first),
        _const_spec((1, D)),
        _const_spec(w_up.shape),
        _const_spec(conv_w.shape),
        _const_spec((1, F)),
        _const_spec(w_down.shape),
        _seq_spec(BB, 2, D, first=mod_first),
        _const_spec((1, D)),
    ]
    args = [x1, mod, ln.reshape(1, D), w_up, conv_w, conv_b.reshape(1, F), w_down, modf, lnf.reshape(1, D)]
    if has_hist:
        in_specs.append(_seq_spec(BB, CONV_W - 1, F))
        args.append(conv_state)
    return pl.pallas_call(
        functools.partial(_ffn_kernel, has_hist=has_hist, BB=BB, T=T, NG=NG, final=final),
        grid=(B // BB, S // T),
        in_specs=in_specs,
        out_specs=[x_spec, _seq_spec(BB, CONV_W - 1, F)],
        out_shape=[jax.ShapeDtypeStruct((B, S, D), F32),
                   jax.ShapeDtypeStruct((B, CONV_W - 1, F), F32)],
        scratch_shapes=[pltpu.VMEM((BB, CONV_PAD, F), F32)],
        compiler_params=pltpu.CompilerParams(
            dimension_semantics=("parallel", "arbitrary"), vmem_limit_bytes=VMEM_LIMIT),
        name="ffn_sample" if has_hist else "ffn_prompt",
    )(*args)


def _pick_tile(S, pref):
    t = min(S, pref)
    while S % t:
        t //= 2
    return t


def kernel(x_prompt, x_sample, cache_k, cache_v, state_ret, state_conv, c_prompt, c_sample, norm1_g, norm2_g, w_ada, b_ada, w_in, w_out, attn_sinks, w_up, conv_w, conv_b, w_down, normf_g, w_ada_f, b_ada_f):
    depth = w_in.shape[0]
    B, S, D = x_prompt.shape
    Bs, Ls, _ = x_sample.shape
    assert S % WINDOW == 0 and Ls == CHUNK and cache_k.shape[2] == WINDOW

    c_all = jnp.concatenate([c_prompt, c_sample], axis=0)
    modf = _ada(c_all, w_ada_f, b_ada_f).reshape(B + Bs, 2, D)
    Tm = _pick_tile(S, 2 * ROWS_PER_STEP)
    Tf = _pick_tile(S, 2 * ROWS_PER_STEP)
    RCp = _pick_tile(Tm, 256)
    BBs = _pick_tile(Bs, ROWS_PER_STEP // Ls)

    yp, ys = x_prompt, x_sample
    outs_p, outs_s = [], []
    for l in range(depth):
        mod = _ada(c_all, w_ada[l], b_ada[l]).reshape(B + Bs, 6, D)
        win = w_in[l].astype(BF16)
        wout = w_out[l].astype(BF16)
        wup = w_up[l].astype(BF16)
        wdn = w_down[l].astype(BF16)
        final = l == depth - 1

        x1, ret_p, k_p, v_p = _mixer(yp, mod, 0, norm1_g[l], win, wout, attn_sinks[l], None,
                                     BB=1, T=Tm, RC=RCp, QG=WINDOW, NG=2)
        yp, conv_p = _ffn(x1, mod, 0, norm2_g[l], wup, conv_w[l], conv_b[l], wdn, modf, normf_g,
                          None, BB=1, T=Tf, NG=4, final=final)
        outs_p.append((ret_p, k_p.reshape(B, WINDOW, ATT_KV_HEADS, ATT_HD),
                       v_p.reshape(B, WINDOW, ATT_KV_HEADS, ATT_HD), conv_p))

        hist = (cache_k[l].reshape(Bs, WINDOW, KV_WIDTH), cache_v[l].reshape(Bs, WINDOW, KV_WIDTH),
                state_ret[l])
        x1s, ret_s, k_s, v_s = _mixer(ys, mod, B, norm1_g[l], win, wout, attn_sinks[l], hist,
                                      BB=BBs, T=Ls, RC=Ls, QG=Ls, NG=1)
        ys, conv_s = _ffn(x1s, mod, B, norm2_g[l], wup, conv_w[l], conv_b[l], wdn, modf, normf_g,
                          state_conv[l], BB=BBs, T=Ls, NG=1, final=final)
        outs_s.append((ret_s, k_s.reshape(Bs, Ls, ATT_KV_HEADS, ATT_HD),
                       v_s.reshape(Bs, Ls, ATT_KV_HEADS, ATT_HD), conv_s))

    stack = lambda lst, i: jnp.stack([t[i] for t in lst])
    return (yp, ys,
            stack(outs_p, 0), stack(outs_p, 1), stack(outs_p, 2), stack(outs_p, 3),
            stack(outs_s, 0), stack(outs_s, 1), stack(outs_s, 2), stack(outs_s, 3))
```
